```python
import jax, jax.numpy as jnp
from jax import lax
import numpy as np

D_MODEL = 2048
BATCH = 32
SEQ = 256
DEPTH = 4
DEC_BATCH = 4
DEC_SEQ = 2048
PAST_LEN = 512

GRID_W = 64
N_HEADS = 16
N_KV_HEADS = 4
HEAD_DIM = D_MODEL // N_HEADS
N_GROUPS_Q = N_HEADS // N_KV_HEADS
Q_WIDTH = N_HEADS * HEAD_DIM
KV_WIDTH = N_KV_HEADS * HEAD_DIM
POOL_WINDOWS = (2, 4, 8, 16)
N_POOL_GROUPS = len(POOL_WINDOWS)
POOL_WIDTH = D_MODEL
POOL_GROUP_DIM = POOL_WIDTH // N_POOL_GROUPS
N_BRANCHES = 2
IN_WIDTH = Q_WIDTH + 2 * KV_WIDTH + POOL_WIDTH + N_BRANCHES * D_MODEL
ROPE_AXIS_DIM = HEAD_DIM // 2
ROPE_THETA = 10000.0
Q_BLOCK = 128
D_FF = -(-(8 * D_MODEL // 3) // 256) * 256
N_EXPERTS = 8
TOP_K = 2
D_EXPERT = D_FF
N_DENSE = (DEPTH + 1) // 2
N_MOE = DEPTH // 2
N_MOD = 6
EPS = 1e-6

kernel_name = "hybrid_pool_gqa_diffusion_step"


def rms_norm(x, g):
    xf = x.astype(jnp.float32)
    y = xf * lax.rsqrt(jnp.mean(xf * xf, axis=-1, keepdims=True) + EPS)
    return (y * g.astype(jnp.float32)).astype(x.dtype)


def rope_axis(x, cos, sin):
    half = x.shape[-1] // 2
    x1, x2 = x[..., :half], x[..., half:]
    cos = cos[None, :, None, :].astype(x.dtype)
    sin = sin[None, :, None, :].astype(x.dtype)
    return jnp.concatenate([x1 * cos - x2 * sin, x2 * cos + x1 * sin], axis=-1)


def rope2d(x, rope_tab):
    cos_r, sin_r, cos_c, sin_c = rope_tab
    return jnp.concatenate([rope_axis(x[..., :ROPE_AXIS_DIM], cos_r, sin_r),
                            rope_axis(x[..., ROPE_AXIS_DIM:], cos_c, sin_c)], axis=-1)


def block_attention(q, k, v):
    b, lq = q.shape[0], q.shape[1]
    nb = lq // Q_BLOCK
    qb = q.reshape(b, nb, Q_BLOCK, N_KV_HEADS, N_GROUPS_Q, HEAD_DIM).transpose(1, 0, 2, 3, 4, 5)
    scale = HEAD_DIM ** -0.5

    def one_block(qblk):
        s = jnp.einsum('bqkgd,bskd->bkgqs', qblk, k).astype(jnp.float32) * scale
        p = jax.nn.softmax(s, axis=-1).astype(v.dtype)
        return jnp.einsum('bkgqs,bskd->bqkgd', p, v)

    o = lax.map(one_block, qb)
    return o.transpose(1, 0, 2, 3, 4, 5).reshape(b, lq, N_HEADS, HEAD_DIM)


def pool_mix(u, w_pool, pool_scale):
    b, length, _ = u.shape
    uf = u.astype(jnp.float32)
    cs = jnp.concatenate([jnp.zeros((b, 1, POOL_WIDTH), jnp.float32), jnp.cumsum(uf, axis=1)], axis=1)
    t = jnp.arange(length)
    outs = []
    for gi, w in enumerate(POOL_WINDOWS):
        sl = slice(gi * POOL_GROUP_DIM, (gi + 1) * POOL_GROUP_DIM)
        lo = jnp.clip(t - w // 2, 0, length)
        hi = jnp.clip(t + w // 2, 0, length)
        cs_g = cs[..., sl]
        mean = (cs_g[:, hi] - cs_g[:, lo]) / (hi - lo).astype(jnp.float32)[None, :, None]
        outs.append(mean - uf[..., sl])
    pooled = jnp.stack(outs, axis=2).astype(u.dtype)
    mixed = jnp.einsum('blgc,gcd->blgd', pooled, w_pool).reshape(b, length, POOL_WIDTH)
    return mixed * pool_scale


def token_mix(h, w_in, q_g, k_g, w_pool, pool_scale, w_out, rope_tab, ext_k, ext_v):
    b, length, _ = h.shape
    proj = h @ w_in
    o1 = Q_WIDTH
    o2 = o1 + KV_WIDTH
    o3 = o2 + KV_WIDTH
    o4 = o3 + POOL_WIDTH
    q = rms_norm(proj[..., :o1].reshape(b, length, N_HEADS, HEAD_DIM), q_g)
    k = rms_norm(proj[..., o1:o2].reshape(b, length, N_KV_HEADS, HEAD_DIM), k_g)
    v = proj[..., o2:o3].reshape(b, length, N_KV_HEADS, HEAD_DIM)
    u = proj[..., o3:o4]
    gates = jax.nn.sigmoid(proj[..., o4:].astype(jnp.float32)).astype(h.dtype)
    if rope_tab is None:
        k_all, v_all = k, v
    else:
        q = rope2d(q, rope_tab)
        k_all = jnp.concatenate([rope2d(k, rope_tab), ext_k], axis=1)
        v_all = jnp.concatenate([v, ext_v], axis=1)
    attn = block_attention(q, k_all, v_all).reshape(b, length, Q_WIDTH)
    pool = pool_mix(u, w_pool, pool_scale)
    merged = gates[..., :D_MODEL] * attn + gates[..., D_MODEL:] * pool
    return merged @ w_out, k, v


def dense_swiglu(x, w_gate, w_up, w_down):
    return (jax.nn.silu(x @ w_gate) * (x @ w_up)) @ w_down


def moe_swiglu(x, router_w, w_gate, w_up, w_down):
    b, length, d = x.shape
    xf = x.reshape(b * length, d)
    logits = (xf @ router_w).astype(jnp.float32)
    top_v, top_i = lax.top_k(logits, TOP_K)
    top_w = jax.nn.softmax(top_v, axis=-1)
    gate = jnp.sum(jax.nn.one_hot(top_i, N_EXPERTS, dtype=jnp.float32) * top_w[..., None], axis=1)
    gate = gate.astype(x.dtype)
    y = jnp.zeros_like(xf)
    for e in range(N_EXPERTS):
        y = y + gate[:, e:e + 1] * dense_swiglu(xf, w_gate[e], w_up[e], w_down[e])
    return y.reshape(b, length, d)


def trunk_layer(x, mod, n1, n2, mix_w, ffn_fn, rope_tab, ext_k, ext_v):
    sh1, sc1, g1, sh2, sc2, g2 = jnp.split(mod, N_MOD, axis=-1)
    h = rms_norm(x, n1) * (1 + sc1) + sh1
    mix, k, v = token_mix(h, *mix_w, rope_tab, ext_k, ext_v)
    x = x + g1 * mix
    h2 = rms_norm(x, n2) * (1 + sc2) + sh2
    x = x + g2 * ffn_fn(h2)
    return x, k, v


def setup_inputs(seed: int = 0) -> dict:
    key = jax.random.key(seed)
    ks = jax.random.split(key, 24)
    f32 = jnp.float32

    def nrm(k, shape, scale):
        return jax.random.normal(k, shape, f32) * scale

    return {
        "x_prompt": nrm(ks[0], (BATCH, SEQ, D_MODEL), 1.0),
        "x_sample": nrm(ks[1], (DEC_BATCH, DEC_SEQ, D_MODEL), 1.0),
        "c": nrm(ks[2], (DEC_BATCH, D_MODEL), 1.0),
        "cache_k": nrm(ks[3], (DEC_BATCH, DEPTH, PAST_LEN, N_KV_HEADS, HEAD_DIM), 1.0),
        "cache_v": nrm(ks[4], (DEC_BATCH, DEPTH, PAST_LEN, N_KV_HEADS, HEAD_DIM), 1.0),
        "c_ctx": nrm(ks[5], (D_MODEL,), 1.0),
        "w_mod": nrm(ks[6], (DEPTH, D_MODEL, N_MOD * D_MODEL), D_MODEL ** -0.5),
        "b_mod": nrm(ks[7], (DEPTH, N_MOD * D_MODEL), 0.01),
        "norm1_g": 1.0 + nrm(ks[8], (DEPTH, D_MODEL), 0.02),
        "norm2_g": 1.0 + nrm(ks[9], (DEPTH, D_MODEL), 0.02),
        "w_in": nrm(ks[10], (DEPTH, D_MODEL, IN_WIDTH), D_MODEL ** -0.5),
        "q_norm_g": 1.0 + nrm(ks[11], (DEPTH, HEAD_DIM), 0.02),
        "k_norm_g": 1.0 + nrm(ks[12], (DEPTH, HEAD_DIM), 0.02),
        "w_pool": nrm(ks[13], (DEPTH, N_POOL_GROUPS, POOL_GROUP_DIM, POOL_GROUP_DIM), POOL_GROUP_DIM ** -0.5),
        "pool_scale": 1.0 + nrm(ks[14], (DEPTH, POOL_WIDTH), 0.02),
        "w_out": nrm(ks[15], (DEPTH, D_MODEL, D_MODEL), D_MODEL ** -0.5),
        "dense_w_gate": nrm(ks[16], (N_DENSE, D_MODEL, D_FF), D_MODEL ** -0.5),
        "dense_w_up": nrm(ks[17], (N_DENSE, D_MODEL, D_FF), D_MODEL ** -0.5),
        "dense_w_down": nrm(ks[18], (N_DENSE, D_FF, D_MODEL), D_FF ** -0.5),
        "router_w": nrm(ks[19], (N_MOE, D_MODEL, N_EXPERTS), D_MODEL ** -0.5),
        "moe_w_gate": nrm(ks[20], (N_MOE, N_EXPERTS, D_MODEL, D_EXPERT), D_MODEL ** -0.5),
        "moe_w_up": nrm(ks[21], (N_MOE, N_EXPERTS, D_MODEL, D_EXPERT), D_MODEL ** -0.5),
        "moe_w_down": nrm(ks[22], (N_MOE, N_EXPERTS, D_EXPERT, D_MODEL), D_EXPERT ** -0.5),
        "final_norm_g": 1.0 + nrm(ks[23], (D_MODEL,), 0.02),
    }


def reference(x_prompt, x_sample, c, cache_k, cache_v, c_ctx, w_mod, b_mod, norm1_g, norm2_g, w_in,
              q_norm_g, k_norm_g, w_pool, pool_scale, w_out, dense_w_gate, dense_w_up, dense_w_down,
              router_w, moe_w_gate, moe_w_up, moe_w_down, final_norm_g):
    lat_len = x_sample.shape[1]
    rows = lat_len // GRID_W
    row_pos = jnp.broadcast_to(jnp.arange(rows)[:, None], (rows, GRID_W)).reshape(-1).astype(jnp.float32)
    col_pos = jnp.broadcast_to(jnp.arange(GRID_W)[None, :], (rows, GRID_W)).reshape(-1).astype(jnp.float32)
    inv_freq = ROPE_THETA ** (-jnp.arange(0, ROPE_AXIS_DIM, 2, dtype=jnp.float32) / ROPE_AXIS_DIM)
    ang_r = row_pos[:, None] * inv_freq[None, :]
    ang_c = col_pos[:, None] * inv_freq[None, :]
    rope_tab = (jnp.cos(ang_r), jnp.sin(ang_r), jnp.cos(ang_c), jnp.sin(ang_c))

    xp, xs = x_prompt, x_sample
    new_k, new_v = [], []
    for l in range(DEPTH):
        mod_ctx = (jax.nn.silu(c_ctx) @ w_mod[l] + b_mod[l])[None, None, :]
        mod_lat = (jax.nn.silu(c) @ w_mod[l] + b_mod[l])[:, None, :]
        mix_w = (w_in[l], q_norm_g[l], k_norm_g[l], w_pool[l], pool_scale[l], w_out[l])
        j = l // 2
        if l % 2 == 0:
            def ffn_fn(h, j=j):
                return dense_swiglu(h, dense_w_gate[j], dense_w_up[j], dense_w_down[j])
        else:
            def ffn_fn(h, j=j):
                return moe_swiglu(h, router_w[j], moe_w_gate[j], moe_w_up[j], moe_w_down[j])
        xp, k_ctx, v_ctx = trunk_layer(xp, mod_ctx, norm1_g[l], norm2_g[l], mix_w, ffn_fn, None, None, None)
        new_k.append(k_ctx)
        new_v.append(v_ctx)
        xs, _, _ = trunk_layer(xs, mod_lat, norm1_g[l], norm2_g[l], mix_w, ffn_fn, rope_tab,
                               cache_k[:, l], cache_v[:, l])

    y_prompt = rms_norm(xp, final_norm_g)
    y_sample = rms_norm(xs, final_norm_g)
    new_cache_k = jnp.stack(new_k, axis=1)
    new_cache_v = jnp.stack(new_v, axis=1)
    return (y_prompt, y_sample, new_cache_k, new_cache_v)
```

```python
import functools

import jax
import jax.numpy as jnp
from jax import lax
from jax.experimental import pallas as pl
from jax.experimental.pallas import tpu as pltpu

GRID_W = 64
ROPE_THETA = 10000.0
POOL_WINDOWS = (2, 4, 8, 16)
TOP_K = 2
N_MOD = 6
EPS = 1e-6

LANES = 128
SUBLANES = 8
VMEM_BUDGET = 60000 * 1024
POOL_BLOCK = 256

F32 = jnp.float32
BF16 = jnp.bfloat16


def _params(semantics, vmem_bytes):
    limit = int(min(VMEM_BUDGET, vmem_bytes * 5 // 4 + (4 << 20)))
    return pltpu.CompilerParams(dimension_semantics=semantics, vmem_limit_bytes=limit)


def _tile(n, pref, align):
    t = min(n, pref)
    t -= t % align
    while t > align and n % t:
        t -= align
    assert t >= align and n % t == 0, (n, pref, align)
    return t


def _rms(x):
    return x * lax.rsqrt(jnp.mean(x * x, axis=-1, keepdims=True) + EPS)


def _sigmoid(x):
    return 1.0 / (1.0 + jnp.exp(-x))


def _mod_kernel(c_ref, w_ref, b_ref, o_ref):
    c = c_ref[...]
    s = c * _sigmoid(c)
    o_ref[...] = jnp.dot(s, w_ref[...], preferred_element_type=F32,
                         precision=lax.Precision.HIGHEST) + b_ref[...]


def _modulation(cs, w_mod, b_mod):
    depth, d, n = w_mod.shape
    rows = cs.shape[0]
    tn = _tile(n, 1024, LANES)
    return pl.pallas_call(
        _mod_kernel,
        grid=(depth, n // tn),
        in_specs=[pl.BlockSpec((rows, d), lambda l, j: (0, 0)),
                  pl.BlockSpec((None, d, tn), lambda l, j: (l, 0, j)),
                  pl.BlockSpec((None, 1, tn), lambda l, j: (l, 0, j))],
        out_specs=pl.BlockSpec((None, rows, tn), lambda l, j: (l, 0, j)),
        out_shape=jax.ShapeDtypeStruct((depth, rows, n), F32),
        compiler_params=_params(("arbitrary", "arbitrary"), 2 * d * tn * 4 + 4 * rows * (d + tn) * 4),
    )(cs, w_mod, b_mod.reshape(depth, 1, n))


def _resid_norm_kernel(*refs, has_res, final):
    it = iter(refs)
    x_ref = next(it)
    if has_res:
        o_ref, g_ref = next(it), next(it)
    n_ref = next(it)
    if not final:
        sc_ref, sh_ref = next(it), next(it)
    outs = list(it)
    x = x_ref[...]
    if has_res:
        x = x + g_ref[...] * o_ref[...]
        outs[0][...] = x
        outs = outs[1:]
    y = _rms(x) * n_ref[...]
    if final:
        outs[0][...] = y
    else:
        outs[0][...] = (y * (1.0 + sc_ref[...]) + sh_ref[...]).astype(BF16)


def _group_spec(d, grp):
    return pl.BlockSpec((None, 1, d), lambda i: (grp(i), 0, 0))


def _resid_norm(x, o, g, n, sc, sh, grp, *, final):
    m, d = x.shape
    tm = _tile(m, 512, SUBLANES)
    has_res = o is not None
    row = pl.BlockSpec((tm, d), lambda i: (i, 0))
    vec = pl.BlockSpec((1, d), lambda i: (0, 0))
    args, specs = [x], [row]
    if has_res:
        args += [o, g]
        specs += [row, _group_spec(d, grp)]
    args.append(n.reshape(1, d))
    specs.append(vec)
    if not final:
        args += [sc, sh]
        specs += [_group_spec(d, grp)] * 2
    out_shape, out_specs = [], []
    if has_res:
        out_shape.append(jax.ShapeDtypeStruct((m, d), F32))
        out_specs.append(row)
    out_shape.append(jax.ShapeDtypeStruct((m, d), F32 if final else BF16))
    out_specs.append(row)
    res = pl.pallas_call(
        functools.partial(_resid_norm_kernel, has_res=has_res, final=final),
        grid=(m // tm,), in_specs=specs, out_specs=out_specs, out_shape=out_shape,
        compiler_params=_params(("arbitrary",), 8 * tm * d * 4),
    )(*args)
    return res if has_res else (None, res[0])


def _rope(y, cos, sin):
    half = LANES // 4
    lane = lax.broadcasted_iota(jnp.int32, y.shape, 1)
    partner = jnp.where((lane & (2 * half - 1)) < half,
                        pltpu.roll(y, LANES - half, axis=1), pltpu.roll(y, half, axis=1))
    return y * cos + partner * sin


def _inproj_kernel(h_ref, w_ref, qg_ref, kg_ref, cos_ref, sin_ref,
                   q_ref, k_ref, v_ref, u_ref, gt_ref, kc_ref, vc_ref,
                   *, nct, bounds, hd, scale):
    i, j = pl.program_id(0), pl.program_id(1)
    acc = jnp.dot(h_ref[...], w_ref[...].astype(BF16), preferred_element_type=F32)
    heads = acc.shape[1] // hd
    jq, jk, jv, ju = bounds
    is_ctx = i < nct

    def normed(gain, rope, mult):
        cols = []
        for hh in range(heads):
            y = _rms(acc[:, hh * hd:(hh + 1) * hd]) * gain
            if rope:
                y = _rope(y, cos_ref[...], sin_ref[...])
            cols.append(y * mult if mult != 1.0 else y)
        return jnp.concatenate(cols, axis=1) if heads > 1 else cols[0]

    @pl.when((j < jq) & is_ctx)
    def _():
        q_ref[...] = normed(qg_ref[...], False, scale).astype(BF16)

    @pl.when((j < jq) & jnp.logical_not(is_ctx))
    def _():
        q_ref[...] = normed(qg_ref[...], True, scale).astype(BF16)

    @pl.when((j >= jq) & (j < jk) & is_ctx)
    def _():
        kn = normed(kg_ref[...], False, 1.0)
        k_ref[...] = kn.astype(BF16)
        kc_ref[...] = kn

    @pl.when((j >= jq) & (j < jk) & jnp.logical_not(is_ctx))
    def _():
        k_ref[...] = normed(kg_ref[...], True, 1.0).astype(BF16)

    @pl.when((j >= jk) & (j < jv))
    def _():
        v_ref[...] = acc.astype(BF16)

    @pl.when((j >= jk) & (j < jv) & is_ctx)
    def _():
        vc_ref[...] = acc

    @pl.when((j >= jv) & (j < ju))
    def _():
        u_ref[...] = acc.astype(BF16)

    @pl.when(j >= ju)
    def _():
        gt_ref[...] = _sigmoid(acc).astype(BF16)


def _inproj(h, w_in, layer, q_g, k_g, cos, sin, dims):
    m, d = h.shape
    n_ctx, dec_seq, qw, kvw, pw, hd = (dims[k] for k in ("n_ctx", "dec_seq", "qw", "kvw", "pw", "hd"))
    n_in = w_in.shape[2]
    gw = n_in - qw - 2 * kvw - pw
    tn = _tile(kvw, 512, LANES)
    tm = _tile(dec_seq, 1024, SUBLANES)
    assert n_ctx % tm == 0 and qw % tn == 0 and pw % tn == 0 and gw % tn == 0
    nct, tps = n_ctx // tm, dec_seq // tm
    jq, jk, jv, ju = qw // tn, (qw + kvw) // tn, (qw + 2 * kvw) // tn, (qw + 2 * kvw + pw) // tn

    def region(lo, hi):
        return lambda i, j: (i, jnp.clip(j - lo, 0, hi - lo - 1))

    def ctx_region(lo, hi):
        return lambda i, j: (jnp.minimum(i, nct - 1), jnp.clip(j - lo, 0, hi - lo - 1))

    tab = pl.BlockSpec((tm, hd), lambda i, j: (jnp.maximum(i - nct, 0) % tps, 0))
    gain = pl.BlockSpec((1, hd), lambda i, j: (0, 0))
    blk = (tm, tn)
    out_shape = [jax.ShapeDtypeStruct((m, w), BF16) for w in (qw, kvw, kvw, pw, gw)]
    out_shape += [jax.ShapeDtypeStruct((n_ctx, kvw), F32)] * 2
    out_specs = [pl.BlockSpec(blk, region(0, jq)), pl.BlockSpec(blk, region(jq, jk)),
                 pl.BlockSpec(blk, region(jk, jv)), pl.BlockSpec(blk, region(jv, ju)),
                 pl.BlockSpec(blk, region(ju, n_in // tn)),
                 pl.BlockSpec(blk, ctx_region(jq, jk)), pl.BlockSpec(blk, ctx_region(jk, jv))]
    vmem = 2 * (tm * d * 2 + d * tn * 4 + 2 * tm * hd * 4 + 5 * tm * tn * 2 + 2 * tm * tn * 4) + 6 * tm * tn * 4
    return pl.pallas_call(
        functools.partial(_inproj_kernel, nct=nct, bounds=(jq, jk, jv, ju), hd=hd, scale=float(hd) ** -0.5),
        grid=(m // tm, n_in // tn),
        in_specs=[pl.BlockSpec((tm, d), lambda i, j: (i, 0)),
                  pl.BlockSpec((None, d, tn), lambda i, j: (layer, 0, j)),
                  gain, gain, tab, tab],
        out_specs=out_specs, out_shape=out_shape,
        compiler_params=_params(("arbitrary", "arbitrary"), vmem),
    )(h, w_in, q_g.reshape(1, hd), k_g.reshape(1, hd), cos, sin)


def _qk(q, k):
    return lax.dot_general(q, k, (((1,), (1,)), ((), ())), preferred_element_type=F32)


def _attn_ctx_kernel(q_ref, k_ref, v_ref, o_ref, *, n_kv, n_g, hd):
    for kv in range(n_kv):
        k = k_ref[:, kv * hd:(kv + 1) * hd]
        v = v_ref[:, kv * hd:(kv + 1) * hd]
        for g in range(n_g):
            c0 = (kv * n_g + g) * hd
            s = _qk(q_ref[:, c0:c0 + hd], k)
            p = jnp.exp(s - jnp.max(s, axis=-1, keepdims=True))
            o = jnp.dot(p.astype(BF16), v, preferred_element_type=F32)
            o_ref[:, c0:c0 + hd] = (o / jnp.sum(p, axis=-1, keepdims=True)).astype(BF16)


def _attn_lat_kernel(q_ref, k_ref, v_ref, ck_ref, cv_ref, o_ref, *, n_g, hd):
    k, v = k_ref[...], v_ref[...]
    ck, cv = ck_ref[...].astype(BF16), cv_ref[...].astype(BF16)
    for g in range(n_g):
        q = q_ref[:, g * hd:(g + 1) * hd]
        s1, s2 = _qk(q, k), _qk(q, ck)
        mx = jnp.maximum(jnp.max(s1, axis=-1, keepdims=True), jnp.max(s2, axis=-1, keepdims=True))
        p1, p2 = jnp.exp(s1 - mx), jnp.exp(s2 - mx)
        den = jnp.sum(p1, axis=-1, keepdims=True) + jnp.sum(p2, axis=-1, keepdims=True)
        o = (jnp.dot(p1.astype(BF16), v, preferred_element_type=F32)
             + jnp.dot(p2.astype(BF16), cv, preferred_element_type=F32))
        o_ref[:, g * hd:(g + 1) * hd] = (o / den).astype(BF16)


def _attention(q, k, v, cache_k, cache_v, layer, dims):
    m, qw = q.shape
    n_ctx, seq, dec_seq, kvw, hd = (dims[key] for key in ("n_ctx", "seq", "dec_seq", "kvw", "hd"))
    n_kv = kvw // hd
    n_g = qw // kvw
    dec_batch, _, past = cache_k.shape[:3]
    ck = cache_k.reshape(dec_batch, cache_k.shape[1], past, kvw)
    cv = cache_v.reshape(dec_batch, cache_v.shape[1], past, kvw)

    ctx = pl.pallas_call(
        functools.partial(_attn_ctx_kernel, n_kv=n_kv, n_g=n_g, hd=hd),
        grid=(n_ctx // seq,),
        in_specs=[pl.BlockSpec((seq, qw), lambda b: (b, 0)),
                  pl.BlockSpec((seq, kvw), lambda b: (b, 0)),
                  pl.BlockSpec((seq, kvw), lambda b: (b, 0))],
        out_specs=pl.BlockSpec((seq, qw), lambda b: (b, 0)),
        out_shape=jax.ShapeDtypeStruct((n_ctx, qw), BF16),
        compiler_params=_params(("arbitrary",), 4 * seq * (qw + kvw) * 2 + 8 * seq * seq * 4),
    )(q, k, v)

    assert n_ctx % dec_seq == 0
    tq = _tile(dec_seq, 256, SUBLANES)
    gw = n_g * hd
    qrow = lambda b, kv, t: ((n_ctx + b * dec_seq) // tq + t, kv)
    krow = lambda b, kv, t: (n_ctx // dec_seq + b, kv)
    crow = lambda b, kv, t: (b, layer, 0, kv)
    lat = pl.pallas_call(
        functools.partial(_attn_lat_kernel, n_g=n_g, hd=hd),
        grid=(dec_batch, n_kv, dec_seq // tq),
        in_specs=[pl.BlockSpec((tq, gw), qrow),
                  pl.BlockSpec((dec_seq, hd), krow), pl.BlockSpec((dec_seq, hd), krow),
                  pl.BlockSpec((None, None, past, hd), crow), pl.BlockSpec((None, None, past, hd), crow)],
        out_specs=pl.BlockSpec((tq, gw), lambda b, kv, t: (b * (dec_seq // tq) + t, kv)),
        out_shape=jax.ShapeDtypeStruct((m - n_ctx, qw), BF16),
        compiler_params=_params(("arbitrary",) * 3,
                                4 * tq * gw * 2 + 4 * dec_seq * hd * 2 + 4 * past * hd * 4
                                + 6 * tq * (dec_seq + past) * 4),
    )(q, k, v, ck, cv)
    return jnp.concatenate([ctx, lat], axis=0)


def _pool_kernel(u_ref, band_ref, w_ref, sc_ref, o_ref, *, nct, seq, dec_seq, windows):
    i, g = pl.program_id(0), pl.program_id(1)
    rows = u_ref.shape[0]
    pb = band_ref.shape[-1]
    w = w_ref[...].astype(BF16)
    half = jnp.zeros((), jnp.int32)
    for gi, win in enumerate(windows):
        half = jnp.where(g == gi, win // 2, half)

    def run(length):
        for b in range(rows // pb):
            p0 = (b * pb) % length
            cur = u_ref[b * pb:(b + 1) * pb, :]
            tot = jnp.dot(band_ref[1], cur, preferred_element_type=F32)
            if p0 > 0:
                tot += jnp.dot(band_ref[0], u_ref[(b - 1) * pb:b * pb, :], preferred_element_type=F32)
            if p0 + pb < length:
                tot += jnp.dot(band_ref[2], u_ref[(b + 1) * pb:(b + 2) * pb, :], preferred_element_type=F32)
            pos = p0 + lax.broadcasted_iota(jnp.int32, (pb, 1), 0)
            cnt = jnp.minimum(pos + half, length) - jnp.maximum(pos - half, 0)
            pooled = tot / cnt.astype(F32) - cur.astype(F32)
            mixed = jnp.dot(pooled.astype(BF16), w, preferred_element_type=F32)
            o_ref[b * pb:(b + 1) * pb, :] = (mixed * sc_ref[...]).astype(BF16)

    @pl.when(i < nct)
    def _():
        run(seq)

    @pl.when(i >= nct)
    def _():
        run(dec_seq)


def _pool_bands(pb):
    t = jnp.arange(pb)[:, None]
    j = jnp.arange(pb)[None, :]
    out = []
    for win in POOL_WINDOWS:
        h = win // 2
        prev = (j - pb - t) >= -h
        cur = ((j - t) >= -h) & ((j - t) <= h - 1)
        nxt = (j + pb - t) <= h - 1
        out.append(jnp.stack([prev, cur, nxt]))
    return jnp.stack(out).astype(BF16)


def _pool(u, bands, w_pool, pool_scale, layer, dims):
    m, pw = u.shape
    n_ctx, seq, dec_seq = dims["n_ctx"], dims["seq"], dims["dec_seq"]
    ng = len(POOL_WINDOWS)
    gd = pw // ng
    pb = bands.shape[-1]
    tr = dec_seq
    assert n_ctx % tr == 0 and tr % seq == 0 and seq % pb == 0 and max(POOL_WINDOWS) // 2 <= pb
    return pl.pallas_call(
        functools.partial(_pool_kernel, nct=n_ctx // tr, seq=seq, dec_seq=dec_seq, windows=POOL_WINDOWS),
        grid=(m // tr, ng),
        in_specs=[pl.BlockSpec((tr, gd), lambda i, g: (i, g)),
                  pl.BlockSpec((None, 3, pb, pb), lambda i, g: (g, 0, 0, 0)),
                  pl.BlockSpec((None, None, gd, gd), lambda i, g: (layer, g, 0, 0)),
                  pl.BlockSpec((1, gd), lambda i, g: (0, g))],
        out_specs=pl.BlockSpec((tr, gd), lambda i, g: (i, g)),
        out_shape=jax.ShapeDtypeStruct((m, pw), BF16),
        compiler_params=_params(("arbitrary", "arbitrary"),
                                4 * tr * gd * 2 + 2 * gd * gd * 4 + 6 * pb * pb * 2 + 8 * pb * gd * 4),
    )(u, bands, w_pool, pool_scale.reshape(1, pw))


def _outproj_kernel(*refs, n_exp):
    (a_ref, p_ref, ga_ref, gb_ref, x_ref, w_ref, g1_ref, n2_ref, sc_ref, sh_ref) = refs[:10]
    rest = refs[10:]
    merged = (ga_ref[...].astype(F32) * a_ref[...].astype(F32)
              + gb_ref[...].astype(F32) * p_ref[...].astype(F32)).astype(BF16)
    mix = jnp.dot(merged, w_ref[...], preferred_element_type=F32)
    x = x_ref[...] + g1_ref[...] * mix
    h2 = _rms(x) * n2_ref[...] * (1.0 + sc_ref[...]) + sh_ref[...]
    if n_exp == 0:
        xo_ref, h_ref = rest
        xo_ref[...] = x
        h_ref[...] = h2.astype(BF16)
        return
    rw_ref, xo_ref, h_ref, rt_ref = rest
    xo_ref[...] = x
    h_ref[...] = h2
    logits = jnp.dot(h2, rw_ref[...], preferred_element_type=F32, precision=lax.Precision.HIGHEST)
    lane = lax.broadcasted_iota(jnp.int32, logits.shape, 1)
    lane_f = lane.astype(F32)
    neg = jnp.float32(-jnp.inf)
    lg = jnp.where(lane < n_exp, logits, neg)
    m1 = jnp.max(lg, axis=-1, keepdims=True)
    i1 = jnp.min(jnp.where(lg == m1, lane_f, float(LANES)), axis=-1, keepdims=True)
    lg2 = jnp.where(lane_f == i1, neg, lg)
    m2 = jnp.max(lg2, axis=-1, keepdims=True)
    i2 = jnp.min(jnp.where(lg2 == m2, lane_f, float(LANES)), axis=-1, keepdims=True)
    e2 = jnp.exp(m2 - m1)
    den = 1.0 + e2
    rt = jnp.where(lane == 0, i1, 0.0)
    rt = jnp.where(lane == 1, i2, rt)
    rt = jnp.where(lane == 2, 1.0 / den, rt)
    rt_ref[...] = jnp.where(lane == 3, e2 / den, rt)


def _outproj(attn, pool, gates, x, w_out_b, g1, n2, sc2, sh2, grp, router_w):
    m, d = x.shape
    tm = _tile(m, 256, SUBLANES)
    row = pl.BlockSpec((tm, d), lambda i: (i, 0))
    vec = pl.BlockSpec((1, d), lambda i: (0, 0))
    args = [attn, pool, gates, gates, x, w_out_b, g1, n2.reshape(1, d), sc2, sh2]
    specs = [row, row, row, pl.BlockSpec((tm, d), lambda i: (i, 1)), row,
             pl.BlockSpec((d, d), lambda i: (0, 0)), _group_spec(d, grp), vec,
             _group_spec(d, grp), _group_spec(d, grp)]
    out_shape = [jax.ShapeDtypeStruct((m, d), F32)]
    out_specs = [row]
    n_exp = 0
    if router_w is None:
        out_shape.append(jax.ShapeDtypeStruct((m, d), BF16))
        out_specs.append(row)
    else:
        n_exp = router_w.shape[1]
        assert TOP_K == 2 and 2 * TOP_K <= LANES and n_exp <= LANES
        args.append(jnp.pad(router_w, ((0, 0), (0, LANES - n_exp))))
        specs.append(pl.BlockSpec((d, LANES), lambda i: (0, 0)))
        out_shape += [jax.ShapeDtypeStruct((m, d), F32), jax.ShapeDtypeStruct((m, LANES), F32)]
        out_specs += [row, pl.BlockSpec((tm, LANES), lambda i: (i, 0))]
    return pl.pallas_call(
        functools.partial(_outproj_kernel, n_exp=n_exp),
        grid=(m // tm,), in_specs=specs, out_specs=out_specs, out_shape=out_shape,
        compiler_params=_params(("arbitrary",), 2 * tm * d * (4 * 2 + 3 * 4) + 2 * d * d * 2 + 8 * tm * d * 4),
    )(*args)


def _changed(te_ref, i):
    return (i == 0) | (te_ref[i] != te_ref[jnp.maximum(i - 1, 0)])


def _up_kernel(te_ref, x_ref, wg_ref, wu_ref, o_ref, wgb, wub):
    i = pl.program_id(1)

    @pl.when(_changed(te_ref, i))
    def _():
        wgb[...] = wg_ref[...].astype(BF16)
        wub[...] = wu_ref[...].astype(BF16)

    x = x_ref[...]
    a = jnp.dot(x, wgb[...], preferred_element_type=F32)
    b = jnp.dot(x, wub[...], preferred_element_type=F32)
    o_ref[...] = (a * _sigmoid(a) * b).astype(BF16)


def _down_kernel(te_ref, a_ref, w_ref, o_ref, wb):
    i = pl.program_id(1)

    @pl.when(_changed(te_ref, i))
    def _():
        wb[...] = w_ref[...].astype(BF16)

    o_ref[...] = jnp.dot(a_ref[...], wb[...], preferred_element_type=F32)


def _swiglu(xs, tile_expert, tm, w_gate, w_up, w_down, sl):
    r, d = xs.shape
    f = w_gate.shape[3]
    tf = _tile(f, 512, LANES)
    tn = _tile(d, 512, LANES)
    nt = r // tm
    act = pl.pallas_call(
        _up_kernel,
        grid_spec=pltpu.PrefetchScalarGridSpec(
            num_scalar_prefetch=1, grid=(f // tf, nt),
            in_specs=[pl.BlockSpec((tm, d), lambda j, i, te: (i, 0)),
                      pl.BlockSpec((None, None, d, tf), lambda j, i, te: (sl, te[i], 0, j)),
                      pl.BlockSpec((None, None, d, tf), lambda j, i, te: (sl, te[i], 0, j))],
            out_specs=pl.BlockSpec((tm, tf), lambda j, i, te: (i, j)),
            scratch_shapes=[pltpu.VMEM((d, tf), BF16), pltpu.VMEM((d, tf), BF16)]),
        out_shape=jax.ShapeDtypeStruct((r, f), BF16),
        compiler_params=_params(("arbitrary", "arbitrary"),
                                2 * (tm * d * 2 + 2 * d * tf * 4 + tm * tf * 2) + 2 * d * tf * 2 + 4 * tm * tf * 4),
    )(tile_expert, xs, w_gate, w_up)
    return pl.pallas_call(
        _down_kernel,
        grid_spec=pltpu.PrefetchScalarGridSpec(
            num_scalar_prefetch=1, grid=(d // tn, nt),
            in_specs=[pl.BlockSpec((tm, f), lambda j, i, te: (i, 0)),
                      pl.BlockSpec((None, None, f, tn), lambda j, i, te: (sl, te[i], 0, j))],
            out_specs=pl.BlockSpec((tm, tn), lambda j, i, te: (i, j)),
            scratch_shapes=[pltpu.VMEM((f, tn), BF16)]),
        out_shape=jax.ShapeDtypeStruct((r, d), F32),
        compiler_params=_params(("arbitrary", "arbitrary"),
                                2 * (tm * f * 2 + f * tn * 4 + tm * tn * 4) + f * tn * 2 + 2 * tm * tn * 4),
    )(tile_expert, act, w_down)


def _row_copy(src_hbm, src_row, dst, dst_row, sem, chunks):
    return pltpu.make_async_copy(
        src_hbm.at[pl.ds(pl.multiple_of(src_row * chunks, chunks), chunks), :],
        dst.at[pl.ds(pl.multiple_of(dst_row * chunks, chunks), chunks), :], sem)


def _gather_kernel(rt_ref, h_hbm, o_ref, buf, sem, *, chunks):
    i = pl.program_id(0)
    tg = o_ref.shape[0]

    def start(r, c):
        _row_copy(h_hbm, rt_ref[i * tg + r], buf, r, sem, chunks).start()
        return c

    def wait(r, c):
        _row_copy(h_hbm, 0, buf, r, sem, chunks).wait()
        return c

    lax.fori_loop(0, tg, start, 0)
    lax.fori_loop(0, tg, wait, 0)
    for s in range(chunks):
        o_ref[:, s * LANES:(s + 1) * LANES] = buf[pl.ds(s, tg, stride=chunks), :].astype(BF16)


def _gather_rows(h_tok, row_token, d):
    chunks = d // LANES
    r = row_token.shape[0]
    tg = _tile(r, 256, SUBLANES)
    return pl.pallas_call(
        functools.partial(_gather_kernel, chunks=chunks),
        grid_spec=pltpu.PrefetchScalarGridSpec(
            num_scalar_prefetch=1, grid=(r // tg,),
            in_specs=[pl.BlockSpec(memory_space=pl.ANY)],
            out_specs=pl.BlockSpec((tg, d), lambda i, rt: (i, 0)),
            scratch_shapes=[pltpu.VMEM((tg * chunks, LANES), F32), pltpu.SemaphoreType.DMA(())]),
        out_shape=jax.ShapeDtypeStruct((r, d), BF16),
        compiler_params=_params(("arbitrary",), tg * d * 4 + 2 * tg * d * 2 + 2 * tg * d * 4),
    )(row_token, h_tok)


def _combine_kernel(*refs, chunks, final):
    pos_ref, o_hbm, rt_ref, x_ref, g_ref, n_ref = refs[:6]
    rest = list(refs[6:])
    if not final:
        sc_ref, sh_ref = rest[:2]
        rest = rest[2:]
    xo_ref, y_ref, b0, b1, sems = rest
    i = pl.program_id(0)
    tc = x_ref.shape[0]

    def start(r, c):
        t = i * tc + r
        _row_copy(o_hbm, pos_ref[TOP_K * t], b0, r, sems.at[0], chunks).start()
        _row_copy(o_hbm, pos_ref[TOP_K * t + 1], b1, r, sems.at[1], chunks).start()
        return c

    def wait(r, c):
        _row_copy(o_hbm, 0, b0, r, sems.at[0], chunks).wait()
        _row_copy(o_hbm, 0, b1, r, sems.at[1], chunks).wait()
        return c

    lax.fori_loop(0, tc, start, 0)
    lax.fori_loop(0, tc, wait, 0)
    w0 = rt_ref[:, TOP_K:TOP_K + 1]
    w1 = rt_ref[:, TOP_K + 1:TOP_K + 2]
    cols = [w0 * b0[pl.ds(s, tc, stride=chunks), :] + w1 * b1[pl.ds(s, tc, stride=chunks), :]
            for s in range(chunks)]
    x = x_ref[...] + g_ref[...] * jnp.concatenate(cols, axis=1)
    xo_ref[...] = x
    y = _rms(x) * n_ref[...]
    if final:
        y_ref[...] = y
    else:
        y_ref[...] = (y * (1.0 + sc_ref[...]) + sh_ref[...]).astype(BF16)


def _combine(o_tok, pos, route, x, g, n, sc, sh, grp, *, final):
    m, d = x.shape
    chunks = d // LANES
    tc = _tile(m, 256, SUBLANES)
    row = pl.BlockSpec((tc, d), lambda i, p: (i, 0))
    gspec = pl.BlockSpec((None, 1, d), lambda i, p: (grp(i), 0, 0))
    args = [pos, o_tok, route, x, g, n.reshape(1, d)]
    specs = [pl.BlockSpec(memory_space=pl.ANY), pl.BlockSpec((tc, LANES), lambda i, p: (i, 0)), row, gspec,
             pl.BlockSpec((1, d), lambda i, p: (0, 0))]
    if not final:
        args += [sc, sh]
        specs += [gspec, gspec]
    return pl.pallas_call(
        functools.partial(_combine_kernel, chunks=chunks, final=final),
        grid_spec=pltpu.PrefetchScalarGridSpec(
            num_scalar_prefetch=1, grid=(m // tc,), in_specs=specs,
            out_specs=[row, row],
            scratch_shapes=[pltpu.VMEM((tc * chunks, LANES), F32), pltpu.VMEM((tc * chunks, LANES), F32),
                            pltpu.SemaphoreType.DMA((TOP_K,))]),
        out_shape=[jax.ShapeDtypeStruct((m, d), F32), jax.ShapeDtypeStruct((m, d), F32 if final else BF16)],
        compiler_params=_params(("arbitrary",), 2 * tc * d * 4 + 6 * tc * d * 4 + 6 * tc * d * 4),
    )(*args)


def _route_plan(route, n_exp, tm):
    m = route.shape[0]
    flat_e = route[:, :TOP_K].astype(jnp.int32).reshape(-1)
    onehot = (flat_e[:, None] == jnp.arange(n_exp)[None, :]).astype(jnp.int32)
    csum = jnp.cumsum(onehot, axis=0)
    rank = jnp.take_along_axis(csum, flat_e[:, None], axis=1)[:, 0] - 1
    counts = csum[-1]
    padded = (counts + tm - 1) // tm * tm
    starts = jnp.cumsum(padded) - padded
    pos = starts[flat_e] + rank
    n_rows = TOP_K * m + n_exp * tm
    row_token = jnp.zeros((n_rows,), jnp.int32).at[pos].set(jnp.arange(TOP_K * m, dtype=jnp.int32) // TOP_K)
    tiles = jnp.arange(n_rows // tm, dtype=jnp.int32)
    tile_expert = jnp.sum(tiles[:, None] * tm >= starts[None, :], axis=1).astype(jnp.int32) - 1
    return pos.astype(jnp.int32), row_token, tile_expert


def _rope_tables(dec_seq, hd):
    half = hd // 4
    rows = dec_seq // GRID_W
    row_pos = jnp.broadcast_to(jnp.arange(rows)[:, None], (rows, GRID_W)).reshape(-1).astype(F32)
    col_pos = jnp.broadcast_to(jnp.arange(GRID_W)[None, :], (rows, GRID_W)).reshape(-1).astype(F32)
    inv_freq = ROPE_THETA ** (-jnp.arange(0, 2 * half, 2, dtype=F32) / (2 * half))
    ang_r = row_pos[:, None] * inv_freq[None, :]
    ang_c = col_pos[:, None] * inv_freq[None, :]
    cos = jnp.concatenate([jnp.cos(ang_r)] * 2 + [jnp.cos(ang_c)] * 2, axis=1)
    sin = jnp.concatenate([-jnp.sin(ang_r), jnp.sin(ang_r), -jnp.sin(ang_c), jnp.sin(ang_c)], axis=1)
    return cos, sin


def kernel(x_prompt, x_sample, c, cache_k, cache_v, c_ctx, w_mod, b_mod, norm1_g, norm2_g, w_in,
           q_norm_g, k_norm_g, w_pool, pool_scale, w_out, dense_w_gate, dense_w_up, dense_w_down,
           router_w, moe_w_gate, moe_w_up, moe_w_down, final_norm_g):
    batch, seq, d = x_prompt.shape
    dec_batch, dec_seq, _ = x_sample.shape
    depth = w_mod.shape[0]
    n_kv, hd = cache_k.shape[3], cache_k.shape[4]
    kvw = n_kv * hd
    qw = d
    pw = pool_scale.shape[1]
    n_ctx = batch * seq
    m = n_ctx + dec_batch * dec_seq
    dims = dict(n_ctx=n_ctx, seq=seq, dec_seq=dec_seq, qw=qw, kvw=kvw, pw=pw, hd=hd)
    n_exp = router_w.shape[2]
    assert hd == LANES and d % LANES == 0

    def grp_for(tile_rows):
        return lambda i: jnp.maximum((i * tile_rows - n_ctx) // dec_seq + 1, 0)

    n_grp = 1 + dec_batch
    rows = -(-n_grp // SUBLANES) * SUBLANES
    cs = jnp.zeros((rows, d), F32).at[0].set(c_ctx).at[1:n_grp].set(c)
    mod = _modulation(cs, w_mod, b_mod)[:, :n_grp]
    mod = mod.reshape(depth, n_grp, N_MOD, 1, d).transpose(0, 2, 1, 3, 4)

    cos, sin = _rope_tables(dec_seq, hd)
    bands = _pool_bands(min(POOL_BLOCK, seq))
    w_out_b = w_out.astype(BF16)

    x = jnp.concatenate([x_prompt.reshape(n_ctx, d), x_sample.reshape(-1, d)], axis=0)
    tm_rn = _tile(m, 512, SUBLANES)
    tm_op = _tile(m, 256, SUBLANES)
    _, h = _resid_norm(x, None, None, norm1_g[0], mod[0, 1], mod[0, 0], grp_for(tm_rn), final=False)

    new_k, new_v = [], []
    for l in range(depth):
        sh1, sc1, g1, sh2, sc2, g2 = (mod[l, k] for k in range(N_MOD))
        q, k, v, u, gates, kc, vc = _inproj(h, w_in, l, q_norm_g[l], k_norm_g[l], cos, sin, dims)
        new_k.append(kc.reshape(batch, seq, n_kv, hd))
        new_v.append(vc.reshape(batch, seq, n_kv, hd))
        attn = _attention(q, k, v, cache_k, cache_v, l, dims)
        pool = _pool(u, bands, w_pool, pool_scale[l:l + 1], l, dims)
        last = l == depth - 1
        if last:
            n_next, sc_next, sh_next = final_norm_g, None, None
        else:
            n_next, sc_next, sh_next = norm1_g[l + 1], mod[l + 1, 1], mod[l + 1, 0]
        j = l // 2
        if l % 2 == 0:
            x, h2 = _outproj(attn, pool, gates, x, w_out_b[l], g1, norm2_g[l], sc2, sh2, grp_for(tm_op), None)
            tm = _tile(m, 1024, SUBLANES)
            o = _swiglu(h2, jnp.zeros((m // tm,), jnp.int32), tm,
                        dense_w_gate[:, None], dense_w_up[:, None], dense_w_down[:, None], j)
            x, h = _resid_norm(x, o, g2, n_next, sc_next, sh_next, grp_for(tm_rn), final=last)
        else:
            x, h2, route = _outproj(attn, pool, gates, x, w_out_b[l], g1, norm2_g[l], sc2, sh2,
                                    grp_for(tm_op), router_w[j])
            tm = _tile(m, 512, SUBLANES)
            pos, row_token, tile_expert = _route_plan(route, n_exp, tm)
            chunks = d // LANES
            xs = _gather_rows(h2.reshape(m * chunks, LANES), row_token, d)
            o = _swiglu(xs, tile_expert, tm, moe_w_gate, moe_w_up, moe_w_down, j)
            x, h = _combine(o.reshape(-1, LANES), pos, route, x, g2, n_next, sc_next, sh_next,
                            grp_for(tm_op), final=last)

    y_prompt = h[:n_ctx].reshape(batch, seq, d)
    y_sample = h[n_ctx:].reshape(dec_batch, dec_seq, d)
    return (y_prompt, y_sample, jnp.stack(new_k, axis=1), jnp.stack(new_v, axis=1))
```

```python
import functools
import math

import jax
import jax.numpy as jnp
from jax import lax
from jax.experimental import pallas as pl
from jax.experimental.pallas import tpu as pltpu

GRID_W = 64
ROPE_THETA = 10000.0
POOL_WINDOWS = (2, 4, 8, 16)
TOP_K = 2
N_MOD = 6
EPS = 1e-6

LANES = 128
SUBLANES = 8
VMEM_BUDGET = 60000 * 1024
POOL_BLOCK = 256
ROW_CHUNK = 256
DMA_UNROLL = 8

F32 = jnp.float32
BF16 = jnp.bfloat16


def _params(semantics, vmem_bytes):
    limit = int(min(VMEM_BUDGET, vmem_bytes * 5 // 4 + (4 << 20)))
    return pltpu.CompilerParams(dimension_semantics=semantics, vmem_limit_bytes=limit)


def _tile(n, pref, align):
    t = min(n, pref)
    t -= t % align
    while t > align and n % t:
        t -= align
    assert t >= align and n % t == 0, (n, pref, align)
    return t


def _rms(x):
    return x * lax.rsqrt(jnp.mean(x * x, axis=-1, keepdims=True) + EPS)


def _sigmoid(x):
    return 1.0 / (1.0 + jnp.exp(-x))


def _mod_kernel(c_ref, w_ref, b_ref, o_ref):
    c = c_ref[...]
    s = c * _sigmoid(c)
    o_ref[...] = jnp.dot(s, w_ref[...], preferred_element_type=F32,
                         precision=lax.Precision.HIGHEST) + b_ref[...]


def _modulation(cs, w_mod, b_mod):
    depth, d, n = w_mod.shape
    rows = cs.shape[0]
    tn = _tile(n, 1024, LANES)
    return pl.pallas_call(
        _mod_kernel,
        grid=(depth, n // tn),
        in_specs=[pl.BlockSpec((rows, d), lambda l, j: (0, 0)),
                  pl.BlockSpec((None, d, tn), lambda l, j: (l, 0, j)),
                  pl.BlockSpec((None, 1, tn), lambda l, j: (l, 0, j))],
        out_specs=pl.BlockSpec((None, rows, tn), lambda l, j: (l, 0, j)),
        out_shape=jax.ShapeDtypeStruct((depth, rows, n), F32),
        compiler_params=_params(("arbitrary", "arbitrary"), 2 * d * tn * 4 + 4 * rows * (d + tn) * 4),
        name="modulation",
    )(cs, w_mod, b_mod.reshape(depth, 1, n))


def _resid_norm_kernel(*refs, has_res, final):
    it = iter(refs)
    x_ref = next(it)
    if has_res:
        o_ref, g_ref = next(it), next(it)
    n_ref = next(it)
    if not final:
        sc_ref, sh_ref = next(it), next(it)
    outs = list(it)
    x = x_ref[...]
    if has_res:
        x = x + g_ref[...] * o_ref[...]
        outs[0][...] = x
        outs = outs[1:]
    y = _rms(x) * n_ref[...]
    if final:
        outs[0][...] = y
    else:
        outs[0][...] = (y * (1.0 + sc_ref[...]) + sh_ref[...]).astype(BF16)


def _group_spec(d, grp):
    return pl.BlockSpec((None, 1, d), lambda i: (grp(i), 0, 0))


def _resid_norm(x, o, g, n, sc, sh, grp, *, final):
    m, d = x.shape
    tm = _tile(m, 512, SUBLANES)
    has_res = o is not None
    row = pl.BlockSpec((tm, d), lambda i: (i, 0))
    vec = pl.BlockSpec((1, d), lambda i: (0, 0))
    args, specs = [x], [row]
    if has_res:
        args += [o, g]
        specs += [row, _group_spec(d, grp)]
    args.append(n.reshape(1, d))
    specs.append(vec)
    if not final:
        args += [sc, sh]
        specs += [_group_spec(d, grp)] * 2
    out_shape, out_specs = [], []
    if has_res:
        out_shape.append(jax.ShapeDtypeStruct((m, d), F32))
        out_specs.append(row)
    out_shape.append(jax.ShapeDtypeStruct((m, d), F32 if final else BF16))
    out_specs.append(row)
    res = pl.pallas_call(
        functools.partial(_resid_norm_kernel, has_res=has_res, final=final),
        grid=(m // tm,), in_specs=specs, out_specs=out_specs, out_shape=out_shape,
        compiler_params=_params(("arbitrary",), 8 * tm * d * 4),
        name="resid_norm",
    )(*args)
    return res if has_res else (None, res[0])


def _rope(y, cos, sin):
    half = LANES // 4
    lane = lax.broadcasted_iota(jnp.int32, y.shape, 1)
    partner = jnp.where((lane & (2 * half - 1)) < half,
                        pltpu.roll(y, LANES - half, axis=1), pltpu.roll(y, half, axis=1))
    return y * cos + partner * sin


def _proj_chunks(h_ref, w_ref, epilogue):
    w = w_ref[...]
    rows = h_ref.shape[0]
    rsub = _tile(rows, ROW_CHUNK, SUBLANES)
    for rc in range(rows // rsub):
        sl = slice(rc * rsub, (rc + 1) * rsub)
        epilogue(jnp.dot(h_ref[sl, :], w, preferred_element_type=F32), sl)


def _qk_proj_kernel(h_ref, w_ref, g_ref, cos_ref, sin_ref, o_ref, *cache, nct, hd, mult):
    def run(is_ctx):
        def epilogue(acc, sl):
            cols = []
            for hh in range(acc.shape[1] // hd):
                y = _rms(acc[:, hh * hd:(hh + 1) * hd]) * g_ref[...]
                if not is_ctx:
                    y = _rope(y, cos_ref[sl, :], sin_ref[sl, :])
                cols.append(y)
            y = jnp.concatenate(cols, axis=1) if len(cols) > 1 else cols[0]
            if is_ctx and cache:
                cache[0][sl, :] = y
            o_ref[sl, :] = (y * mult if mult != 1.0 else y).astype(BF16)
        _proj_chunks(h_ref, w_ref, epilogue)

    i = pl.program_id(0)
    pl.when(i < nct)(lambda: run(True))
    pl.when(i >= nct)(lambda: run(False))


def _plain_proj_kernel(h_ref, w_ref, o_ref, *cache, nct, gate):
    def run(to_cache):
        def epilogue(acc, sl):
            o_ref[sl, :] = (_sigmoid(acc) if gate else acc).astype(BF16)
            if to_cache:
                cache[0][sl, :] = acc
        _proj_chunks(h_ref, w_ref, epilogue)

    if cache:
        i = pl.program_id(0)
        pl.when(i < nct)(lambda: run(True))
        pl.when(i >= nct)(lambda: run(False))
    else:
        run(False)


def _proj(h, w_in_b, layer, col0, width, dims, *, mode, gain=None, tabs=None, with_cache=False):
    m, d = h.shape
    n_ctx, dec_seq, hd = dims["n_ctx"], dims["dec_seq"], dims["hd"]
    tn = _tile(math.gcd(width, col0), 512, LANES)
    tm = _tile(dec_seq, 1024, SUBLANES)
    assert n_ctx % tm == 0 and col0 % tn == 0 and width % tn == 0
    nct, tps = n_ctx // tm, dec_seq // tm
    args = [h, w_in_b]
    specs = [pl.BlockSpec((tm, d), lambda i, j: (i, 0)),
             pl.BlockSpec((None, d, tn), lambda i, j: (layer, 0, col0 // tn + j))]
    if mode in ("q", "k"):
        tab = pl.BlockSpec((tm, hd), lambda i, j: (jnp.maximum(i - nct, 0) % tps, 0))
        args += [gain.reshape(1, hd), tabs[0], tabs[1]]
        specs += [pl.BlockSpec((1, hd), lambda i, j: (0, 0)), tab, tab]
        body = functools.partial(_qk_proj_kernel, nct=nct, hd=hd,
                                 mult=float(hd) ** -0.5 if mode == "q" else 1.0)
    else:
        body = functools.partial(_plain_proj_kernel, nct=nct, gate=mode == "gate")
    out_shape = [jax.ShapeDtypeStruct((m, width), BF16)]
    out_specs = [pl.BlockSpec((tm, tn), lambda i, j: (i, j))]
    if with_cache:
        out_shape.append(jax.ShapeDtypeStruct((n_ctx, width), F32))
        out_specs.append(pl.BlockSpec((tm, tn), lambda i, j: (jnp.minimum(i, nct - 1), j)))
    vmem = 2 * (tm * d * 2 + d * tn * 2 + 2 * tm * hd * 4 + tm * tn * 2 + tm * tn * 4) + 8 * ROW_CHUNK * tn * 4
    res = pl.pallas_call(
        body, grid=(m // tm, width // tn), in_specs=specs, out_specs=out_specs, out_shape=out_shape,
        compiler_params=_params(("arbitrary", "arbitrary"), vmem), name="proj_" + mode,
    )(*args)
    return res if with_cache else res[0]


def _qk(q, k):
    return lax.dot_general(q, k, (((1,), (1,)), ((), ())), preferred_element_type=F32)


def _attn_ctx_kernel(q_ref, k_ref, v_ref, o_ref, *, n_kv, n_g, hd):
    for kv in range(n_kv):
        k = k_ref[:, kv * hd:(kv + 1) * hd]
        v = v_ref[:, kv * hd:(kv + 1) * hd]
        for g in range(n_g):
            c0 = (kv * n_g + g) * hd
            s = _qk(q_ref[:, c0:c0 + hd], k)
            p = jnp.exp(s - jnp.max(s, axis=-1, keepdims=True))
            o = jnp.dot(p.astype(BF16), v, preferred_element_type=F32)
            o_ref[:, c0:c0 + hd] = (o / jnp.sum(p, axis=-1, keepdims=True)).astype(BF16)


def _attn_lat_kernel(q_ref, k_ref, v_ref, ck_ref, cv_ref, o_ref, *, n_g, hd):
    k, v = k_ref[...], v_ref[...]
    ck, cv = ck_ref[...].astype(BF16), cv_ref[...].astype(BF16)
    for g in range(n_g):
        q = q_ref[:, g * hd:(g + 1) * hd]
        s1, s2 = _qk(q, k), _qk(q, ck)
        mx = jnp.maximum(jnp.max(s1, axis=-1, keepdims=True), jnp.max(s2, axis=-1, keepdims=True))
        p1, p2 = jnp.exp(s1 - mx), jnp.exp(s2 - mx)
        den = jnp.sum(p1, axis=-1, keepdims=True) + jnp.sum(p2, axis=-1, keepdims=True)
        o = (jnp.dot(p1.astype(BF16), v, preferred_element_type=F32)
             + jnp.dot(p2.astype(BF16), cv, preferred_element_type=F32))
        o_ref[:, g * hd:(g + 1) * hd] = (o / den).astype(BF16)


def _attention(q, k, v, cache_k, cache_v, layer, dims):
    m, qw = q.shape
    n_ctx, seq, dec_seq, kvw, hd = (dims[key] for key in ("n_ctx", "seq", "dec_seq", "kvw", "hd"))
    n_kv = kvw // hd
    n_g = qw // kvw
    dec_batch, _, past = cache_k.shape[:3]
    ck = cache_k.reshape(dec_batch, cache_k.shape[1], past, kvw)
    cv = cache_v.reshape(dec_batch, cache_v.shape[1], past, kvw)

    ctx = pl.pallas_call(
        functools.partial(_attn_ctx_kernel, n_kv=n_kv, n_g=n_g, hd=hd),
        grid=(n_ctx // seq,),
        in_specs=[pl.BlockSpec((seq, qw), lambda b: (b, 0)),
                  pl.BlockSpec((seq, kvw), lambda b: (b, 0)),
                  pl.BlockSpec((seq, kvw), lambda b: (b, 0))],
        out_specs=pl.BlockSpec((seq, qw), lambda b: (b, 0)),
        out_shape=jax.ShapeDtypeStruct((n_ctx, qw), BF16),
        compiler_params=_params(("arbitrary",), 4 * seq * (qw + kvw) * 2 + 8 * seq * seq * 4),
        name="attn_ctx",
    )(q, k, v)

    assert n_ctx % dec_seq == 0
    tq = _tile(dec_seq, 256, SUBLANES)
    gw = n_g * hd
    qrow = lambda b, kv, t: ((n_ctx + b * dec_seq) // tq + t, kv)
    krow = lambda b, kv, t: (n_ctx // dec_seq + b, kv)
    crow = lambda b, kv, t: (b, layer, 0, kv)
    lat = pl.pallas_call(
        functools.partial(_attn_lat_kernel, n_g=n_g, hd=hd),
        grid=(dec_batch, n_kv, dec_seq // tq),
        in_specs=[pl.BlockSpec((tq, gw), qrow),
                  pl.BlockSpec((dec_seq, hd), krow), pl.BlockSpec((dec_seq, hd), krow),
                  pl.BlockSpec((None, None, past, hd), crow), pl.BlockSpec((None, None, past, hd), crow)],
        out_specs=pl.BlockSpec((tq, gw), lambda b, kv, t: (b * (dec_seq // tq) + t, kv)),
        out_shape=jax.ShapeDtypeStruct((m - n_ctx, qw), BF16),
        compiler_params=_params(("arbitrary",) * 3,
                                4 * tq * gw * 2 + 4 * dec_seq * hd * 2 + 4 * past * hd * 4
                                + 6 * tq * (dec_seq + past) * 4),
        name="attn_lat",
    )(q, k, v, ck, cv)
    return jnp.concatenate([ctx, lat], axis=0)


def _pool_kernel(u_ref, band_ref, w_ref, sc_ref, o_ref, *, nct, seq, dec_seq, windows):
    i, g = pl.program_id(0), pl.program_id(1)
    rows = u_ref.shape[0]
    pb = band_ref.shape[-1]
    w = w_ref[...].astype(BF16)
    half = jnp.zeros((), jnp.int32)
    for gi, win in enumerate(windows):
        half = jnp.where(g == gi, win // 2, half)

    def run(length):
        for b in range(rows // pb):
            p0 = (b * pb) % length
            cur = u_ref[b * pb:(b + 1) * pb, :]
            tot = jnp.dot(band_ref[1], cur, preferred_element_type=F32)
            if p0 > 0:
                tot += jnp.dot(band_ref[0], u_ref[(b - 1) * pb:b * pb, :], preferred_element_type=F32)
            if p0 + pb < length:
                tot += jnp.dot(band_ref[2], u_ref[(b + 1) * pb:(b + 2) * pb, :], preferred_element_type=F32)
            pos = p0 + lax.broadcasted_iota(jnp.int32, (pb, 1), 0)
            cnt = jnp.minimum(pos + half, length) - jnp.maximum(pos - half, 0)
            pooled = tot / cnt.astype(F32) - cur.astype(F32)
            mixed = jnp.dot(pooled.astype(BF16), w, preferred_element_type=F32)
            o_ref[b * pb:(b + 1) * pb, :] = (mixed * sc_ref[...]).astype(BF16)

    @pl.when(i < nct)
    def _():
        run(seq)

    @pl.when(i >= nct)
    def _():
        run(dec_seq)


def _pool_bands(pb):
    t = jnp.arange(pb)[:, None]
    j = jnp.arange(pb)[None, :]
    out = []
    for win in POOL_WINDOWS:
        h = win // 2
        prev = (j - pb - t) >= -h
        cur = ((j - t) >= -h) & ((j - t) <= h - 1)
        nxt = (j + pb - t) <= h - 1
        out.append(jnp.stack([prev, cur, nxt]))
    return jnp.stack(out).astype(BF16)


def _pool(u, bands, w_pool, pool_scale, layer, dims):
    m, pw = u.shape
    n_ctx, seq, dec_seq = dims["n_ctx"], dims["seq"], dims["dec_seq"]
    ng = len(POOL_WINDOWS)
    gd = pw // ng
    pb = bands.shape[-1]
    tr = dec_seq
    assert n_ctx % tr == 0 and tr % seq == 0 and seq % pb == 0 and max(POOL_WINDOWS) // 2 <= pb
    return pl.pallas_call(
        functools.partial(_pool_kernel, nct=n_ctx // tr, seq=seq, dec_seq=dec_seq, windows=POOL_WINDOWS),
        grid=(m // tr, ng),
        in_specs=[pl.BlockSpec((tr, gd), lambda i, g: (i, g)),
                  pl.BlockSpec((None, 3, pb, pb), lambda i, g: (g, 0, 0, 0)),
                  pl.BlockSpec((None, None, gd, gd), lambda i, g: (layer, g, 0, 0)),
                  pl.BlockSpec((1, gd), lambda i, g: (0, g))],
        out_specs=pl.BlockSpec((tr, gd), lambda i, g: (i, g)),
        out_shape=jax.ShapeDtypeStruct((m, pw), BF16),
        compiler_params=_params(("arbitrary", "arbitrary"),
                                4 * tr * gd * 2 + 2 * gd * gd * 4 + 6 * pb * pb * 2 + 8 * pb * gd * 4),
        name="pool_mix",
    )(u, bands, w_pool, pool_scale.reshape(1, pw))


def _outproj_kernel(*refs, n_exp):
    (a_ref, p_ref, ga_ref, gb_ref, x_ref, w_ref, g1_ref, n2_ref, sc_ref, sh_ref) = refs[:10]
    rest = refs[10:]
    merged = (ga_ref[...].astype(F32) * a_ref[...].astype(F32)
              + gb_ref[...].astype(F32) * p_ref[...].astype(F32)).astype(BF16)
    mix = jnp.dot(merged, w_ref[...], preferred_element_type=F32)
    x = x_ref[...] + g1_ref[...] * mix
    h2 = _rms(x) * n2_ref[...] * (1.0 + sc_ref[...]) + sh_ref[...]
    if n_exp == 0:
        xo_ref, h_ref = rest
        xo_ref[...] = x
        h_ref[...] = h2.astype(BF16)
        return
    rw_ref, xo_ref, h_ref, rt_ref = rest
    xo_ref[...] = x
    rows, d = h2.shape
    chunks = d // LANES
    for s in range(chunks):
        h_ref[pl.ds(s, rows, stride=chunks), :] = h2[:, s * LANES:(s + 1) * LANES]
    logits = jnp.dot(h2, rw_ref[...], preferred_element_type=F32, precision=lax.Precision.HIGHEST)
    lane = lax.broadcasted_iota(jnp.int32, logits.shape, 1)
    lane_f = lane.astype(F32)
    neg = jnp.float32(-jnp.inf)
    lg = jnp.where(lane < n_exp, logits, neg)
    m1 = jnp.max(lg, axis=-1, keepdims=True)
    i1 = jnp.min(jnp.where(lg == m1, lane_f, float(LANES)), axis=-1, keepdims=True)
    lg2 = jnp.where(lane_f == i1, neg, lg)
    m2 = jnp.max(lg2, axis=-1, keepdims=True)
    i2 = jnp.min(jnp.where(lg2 == m2, lane_f, float(LANES)), axis=-1, keepdims=True)
    e2 = jnp.exp(m2 - m1)
    den = 1.0 + e2
    rt = jnp.where(lane == 0, i1, 0.0)
    rt = jnp.where(lane == 1, i2, rt)
    rt = jnp.where(lane == 2, 1.0 / den, rt)
    rt_ref[...] = jnp.where(lane == 3, e2 / den, rt)


def _outproj(attn, pool, gates, x, w_out_b, g1, n2, sc2, sh2, grp, router_w):
    m, d = x.shape
    tm = _tile(m, 256, SUBLANES)
    chunks = d // LANES
    row = pl.BlockSpec((tm, d), lambda i: (i, 0))
    vec = pl.BlockSpec((1, d), lambda i: (0, 0))
    args = [attn, pool, gates, gates, x, w_out_b, g1, n2.reshape(1, d), sc2, sh2]
    specs = [row, row, row, pl.BlockSpec((tm, d), lambda i: (i, 1)), row,
             pl.BlockSpec((d, d), lambda i: (0, 0)), _group_spec(d, grp), vec,
             _group_spec(d, grp), _group_spec(d, grp)]
    out_shape = [jax.ShapeDtypeStruct((m, d), F32)]
    out_specs = [row]
    n_exp = 0
    if router_w is None:
        out_shape.append(jax.ShapeDtypeStruct((m, d), BF16))
        out_specs.append(row)
    else:
        n_exp = router_w.shape[1]
        assert TOP_K == 2 and 2 * TOP_K <= LANES and n_exp <= LANES
        args.append(jnp.pad(router_w, ((0, 0), (0, LANES - n_exp))))
        specs.append(pl.BlockSpec((d, LANES), lambda i: (0, 0)))
        out_shape += [jax.ShapeDtypeStruct((m * chunks, LANES), F32), jax.ShapeDtypeStruct((m, LANES), F32)]
        out_specs += [pl.BlockSpec((tm * chunks, LANES), lambda i: (i, 0)),
                      pl.BlockSpec((tm, LANES), lambda i: (i, 0))]
    return pl.pallas_call(
        functools.partial(_outproj_kernel, n_exp=n_exp),
        grid=(m // tm,), in_specs=specs, out_specs=out_specs, out_shape=out_shape,
        compiler_params=_params(("arbitrary",), 2 * tm * d * (4 * 2 + 3 * 4) + 2 * d * d * 2 + 8 * tm * d * 4),
        name="outproj_route" if n_exp else "outproj",
    )(*args)


def _changed(te_ref, i):
    return (i == 0) | (te_ref[i] != te_ref[jnp.maximum(i - 1, 0)])


def _up_kernel(te_ref, nu_ref, x_ref, wg_ref, wu_ref, o_ref, wgb, wub):
    i = pl.program_id(1)

    @pl.when(_changed(te_ref, i))
    def _():
        wgb[...] = wg_ref[...].astype(BF16)
        wub[...] = wu_ref[...].astype(BF16)

    @pl.when(i < nu_ref[0])
    def _():
        x = x_ref[...]
        a = jnp.dot(x, wgb[...], preferred_element_type=F32)
        b = jnp.dot(x, wub[...], preferred_element_type=F32)
        o_ref[...] = (a * _sigmoid(a) * b).astype(BF16)

    @pl.when(i >= nu_ref[0])
    def _():
        o_ref[...] = jnp.zeros(o_ref.shape, BF16)


def _down_kernel(te_ref, nu_ref, a_ref, w_ref, o_ref, wb):
    i = pl.program_id(1)

    @pl.when(_changed(te_ref, i))
    def _():
        wb[...] = w_ref[...].astype(BF16)

    o_ref[...] = jnp.dot(a_ref[...], wb[...], preferred_element_type=F32)


def _down_tok_kernel(te_ref, nu_ref, a_ref, w_ref, o_ref, *, chunks):
    i, j = pl.program_id(0), pl.program_id(1)
    tm = a_ref.shape[0]
    per = w_ref.shape[1] // LANES

    @pl.when(i < nu_ref[0])
    def _():
        acc = jnp.dot(a_ref[...], w_ref[...], preferred_element_type=F32)
        for jj in range(chunks // per):
            @pl.when(j == jj)
            def _():
                for c in range(per):
                    o_ref[pl.ds(jj * per + c, tm, stride=chunks), :] = acc[:, c * LANES:(c + 1) * LANES]

    @pl.when((i >= nu_ref[0]) & (j == 0))
    def _():
        o_ref[...] = jnp.zeros(o_ref.shape, F32)


def _swiglu(xs, tile_expert, n_used, tm, w_gate, w_up, w_down, sl, *, token_major):
    r, d = xs.shape
    f = w_gate.shape[3]
    tf = _tile(f, 512, LANES)
    tn = _tile(d, 512, LANES)
    nt = r // tm
    used = lambda i, nu: jnp.minimum(i, nu[0] - 1)
    act = pl.pallas_call(
        _up_kernel,
        grid_spec=pltpu.PrefetchScalarGridSpec(
            num_scalar_prefetch=2, grid=(f // tf, nt),
            in_specs=[pl.BlockSpec((tm, d), lambda j, i, te, nu: (used(i, nu), 0)),
                      pl.BlockSpec((None, None, d, tf), lambda j, i, te, nu: (sl, te[i], 0, j)),
                      pl.BlockSpec((None, None, d, tf), lambda j, i, te, nu: (sl, te[i], 0, j))],
            out_specs=pl.BlockSpec((tm, tf), lambda j, i, te, nu: (i, j)),
            scratch_shapes=[pltpu.VMEM((d, tf), BF16), pltpu.VMEM((d, tf), BF16)]),
        out_shape=jax.ShapeDtypeStruct((r, f), BF16),
        compiler_params=_params(("arbitrary", "arbitrary"),
                                2 * (tm * d * 2 + 2 * d * tf * 4 + tm * tf * 2) + 2 * d * tf * 2 + 4 * tm * tf * 4),
        name="swiglu_up",
    )(tile_expert, n_used, xs, w_gate, w_up)
    if not token_major:
        return pl.pallas_call(
            _down_kernel,
            grid_spec=pltpu.PrefetchScalarGridSpec(
                num_scalar_prefetch=2, grid=(d // tn, nt),
                in_specs=[pl.BlockSpec((tm, f), lambda j, i, te, nu: (i, 0)),
                          pl.BlockSpec((None, None, f, tn), lambda j, i, te, nu: (sl, te[i], 0, j))],
                out_specs=pl.BlockSpec((tm, tn), lambda j, i, te, nu: (i, j)),
                scratch_shapes=[pltpu.VMEM((f, tn), BF16)]),
            out_shape=jax.ShapeDtypeStruct((r, d), F32),
            compiler_params=_params(("arbitrary", "arbitrary"),
                                    2 * (tm * f * 2 + f * tn * 4 + tm * tn * 4) + f * tn * 2 + 2 * tm * tn * 4),
            name="swiglu_down",
        )(tile_expert, n_used, act, w_down)
    chunks = d // LANES
    n_j = d // tn
    col = lambda i, j, nu: jnp.where(i < nu[0], j, n_j - 1)
    return pl.pallas_call(
        functools.partial(_down_tok_kernel, chunks=chunks),
        grid_spec=pltpu.PrefetchScalarGridSpec(
            num_scalar_prefetch=2, grid=(nt, n_j),
            in_specs=[pl.BlockSpec((tm, f), lambda i, j, te, nu: (used(i, nu), 0)),
                      pl.BlockSpec((None, None, f, tn), lambda i, j, te, nu: (sl, te[i], 0, col(i, j, nu)))],
            out_specs=pl.BlockSpec((tm * chunks, LANES), lambda i, j, te, nu: (i, 0))),
        out_shape=jax.ShapeDtypeStruct((r * chunks, LANES), F32),
        compiler_params=_params(("arbitrary", "arbitrary"),
                                2 * (tm * f * 2 + f * tn * 2 + tm * d * 4) + 4 * tm * tn * 4),
        name="swiglu_down_tok",
    )(tile_expert, n_used, act, w_down)


def _row_copy(src_hbm, src_row, dst, dst_row, sem, chunks):
    return pltpu.make_async_copy(
        src_hbm.at[pl.ds(pl.multiple_of(src_row * chunks, chunks), chunks), :],
        dst.at[pl.ds(pl.multiple_of(dst_row * chunks, chunks), chunks), :], sem)


def _start_rows(idx_ref, base, stride, n, src_hbm, dst, sem, chunks):
    def body(o, c):
        for u in range(DMA_UNROLL):
            r = o * DMA_UNROLL + u
            _row_copy(src_hbm, idx_ref[base + stride * r], dst, r, sem, chunks).start()
        return c
    lax.fori_loop(0, n // DMA_UNROLL, body, 0)


def _wait_rows(n, src_hbm, dst, sem, chunks):
    def body(o, c):
        for u in range(DMA_UNROLL):
            _row_copy(src_hbm, 0, dst, o * DMA_UNROLL + u, sem, chunks).wait()
        return c
    lax.fori_loop(0, n // DMA_UNROLL, body, 0)


def _gather_kernel(rt_ref, h_hbm, o_ref, buf, sems, *, chunks):
    i, n = pl.program_id(0), pl.num_programs(0)
    tg = o_ref.shape[0]
    slot = i % 2

    @pl.when(i == 0)
    def _():
        _start_rows(rt_ref, 0, 1, tg, h_hbm, buf.at[0], sems.at[0], chunks)

    @pl.when(i + 1 < n)
    def _():
        _start_rows(rt_ref, (i + 1) * tg, 1, tg, h_hbm, buf.at[1 - slot], sems.at[1 - slot], chunks)

    cur = buf.at[slot]
    _wait_rows(tg, h_hbm, cur, sems.at[slot], chunks)
    for s in range(chunks):
        o_ref[:, s * LANES:(s + 1) * LANES] = cur[pl.ds(s, tg, stride=chunks), :].astype(BF16)


def _gather_rows(h_tok, row_token, d):
    chunks = d // LANES
    r = row_token.shape[0]
    tg = _tile(r, 256, SUBLANES * DMA_UNROLL)
    return pl.pallas_call(
        functools.partial(_gather_kernel, chunks=chunks),
        grid_spec=pltpu.PrefetchScalarGridSpec(
            num_scalar_prefetch=1, grid=(r // tg,),
            in_specs=[pl.BlockSpec(memory_space=pl.ANY)],
            out_specs=pl.BlockSpec((tg, d), lambda i, rt: (i, 0)),
            scratch_shapes=[pltpu.VMEM((2, tg * chunks, LANES), F32), pltpu.SemaphoreType.DMA((2,))]),
        out_shape=jax.ShapeDtypeStruct((r, d), BF16),
        compiler_params=_params(("arbitrary",), 2 * tg * d * 4 + 2 * tg * d * 2 + 2 * tg * d * 4),
        name="moe_gather",
    )(row_token, h_tok)


def _combine_kernel(*refs, chunks, final):
    pos_ref, o_hbm, rt_ref, x_ref, g_ref, n_ref = refs[:6]
    rest = list(refs[6:])
    if not final:
        sc_ref, sh_ref = rest[:2]
        rest = rest[2:]
    xo_ref, y_ref, bufs, sems = rest
    i, n = pl.program_id(0), pl.num_programs(0)
    tc = x_ref.shape[0]
    slot = i % 2

    def start(step, sl):
        for k in range(TOP_K):
            _start_rows(pos_ref, TOP_K * step * tc + k, TOP_K, tc, o_hbm, bufs.at[sl, k], sems.at[sl, k], chunks)

    @pl.when(i == 0)
    def _():
        start(0, 0)

    @pl.when(i + 1 < n)
    def _():
        start(i + 1, 1 - slot)

    for k in range(TOP_K):
        _wait_rows(tc, o_hbm, bufs.at[slot, k], sems.at[slot, k], chunks)
    w0 = rt_ref[:, TOP_K:TOP_K + 1]
    w1 = rt_ref[:, TOP_K + 1:TOP_K + 2]
    b0, b1 = bufs.at[slot, 0], bufs.at[slot, 1]
    cols = [w0 * b0[pl.ds(s, tc, stride=chunks), :] + w1 * b1[pl.ds(s, tc, stride=chunks), :]
            for s in range(chunks)]
    x = x_ref[...] + g_ref[...] * jnp.concatenate(cols, axis=1)
    xo_ref[...] = x
    y = _rms(x) * n_ref[...]
    if final:
        y_ref[...] = y
    else:
        y_ref[...] = (y * (1.0 + sc_ref[...]) + sh_ref[...]).astype(BF16)


def _combine(o_tok, pos, route, x, g, n, sc, sh, grp, *, final):
    m, d = x.shape
    chunks = d // LANES
    tc = _tile(m, 256, SUBLANES * DMA_UNROLL)
    row = pl.BlockSpec((tc, d), lambda i, p: (i, 0))
    gspec = pl.BlockSpec((None, 1, d), lambda i, p: (grp(i), 0, 0))
    args = [pos, o_tok, route, x, g, n.reshape(1, d)]
    specs = [pl.BlockSpec(memory_space=pl.ANY), pl.BlockSpec((tc, LANES), lambda i, p: (i, 0)), row, gspec,
             pl.BlockSpec((1, d), lambda i, p: (0, 0))]
    if not final:
        args += [sc, sh]
        specs += [gspec, gspec]
    return pl.pallas_call(
        functools.partial(_combine_kernel, chunks=chunks, final=final),
        grid_spec=pltpu.PrefetchScalarGridSpec(
            num_scalar_prefetch=1, grid=(m // tc,), in_specs=specs,
            out_specs=[row, row],
            scratch_shapes=[pltpu.VMEM((2, TOP_K, tc * chunks, LANES), F32),
                            pltpu.SemaphoreType.DMA((2, TOP_K))]),
        out_shape=[jax.ShapeDtypeStruct((m, d), F32), jax.ShapeDtypeStruct((m, d), F32 if final else BF16)],
        compiler_params=_params(("arbitrary",), 2 * TOP_K * tc * d * 4 + 6 * tc * d * 4 + 6 * tc * d * 4),
        name="moe_combine",
    )(*args)


def _route_plan(route, n_exp, tm):
    m = route.shape[0]
    flat_e = route[:, :TOP_K].astype(jnp.int32).reshape(-1)
    onehot = (flat_e[:, None] == jnp.arange(n_exp)[None, :]).astype(jnp.int32)
    csum = jnp.cumsum(onehot, axis=0)
    rank = jnp.take_along_axis(csum, flat_e[:, None], axis=1)[:, 0] - 1
    counts = csum[-1]
    padded = (counts + tm - 1) // tm * tm
    starts = jnp.cumsum(padded) - padded
    pos = starts[flat_e] + rank
    n_rows = TOP_K * m + n_exp * tm
    row_token = jnp.zeros((n_rows,), jnp.int32).at[pos].set(jnp.arange(TOP_K * m, dtype=jnp.int32) // TOP_K)
    tiles = jnp.arange(n_rows // tm, dtype=jnp.int32)
    tile_expert = jnp.sum(tiles[:, None] * tm >= starts[None, :], axis=1).astype(jnp.int32) - 1
    n_used = (jnp.sum(padded) // tm).astype(jnp.int32).reshape(1)
    return pos.astype(jnp.int32), row_token, tile_expert, n_used


def _rope_tables(dec_seq, hd):
    half = hd // 4
    rows = dec_seq // GRID_W
    row_pos = jnp.broadcast_to(jnp.arange(rows)[:, None], (rows, GRID_W)).reshape(-1).astype(F32)
    col_pos = jnp.broadcast_to(jnp.arange(GRID_W)[None, :], (rows, GRID_W)).reshape(-1).astype(F32)
    inv_freq = ROPE_THETA ** (-jnp.arange(0, 2 * half, 2, dtype=F32) / (2 * half))
    ang_r = row_pos[:, None] * inv_freq[None, :]
    ang_c = col_pos[:, None] * inv_freq[None, :]
    cos = jnp.concatenate([jnp.cos(ang_r)] * 2 + [jnp.cos(ang_c)] * 2, axis=1)
    sin = jnp.concatenate([-jnp.sin(ang_r), jnp.sin(ang_r), -jnp.sin(ang_c), jnp.sin(ang_c)], axis=1)
    return cos, sin


def kernel(x_prompt, x_sample, c, cache_k, cache_v, c_ctx, w_mod, b_mod, norm1_g, norm2_g, w_in,
           q_norm_g, k_norm_g, w_pool, pool_scale, w_out, dense_w_gate, dense_w_up, dense_w_down,
           router_w, moe_w_gate, moe_w_up, moe_w_down, final_norm_g):
    batch, seq, d = x_prompt.shape
    dec_batch, dec_seq, _ = x_sample.shape
    depth = w_mod.shape[0]
    n_kv, hd = cache_k.shape[3], cache_k.shape[4]
    kvw = n_kv * hd
    qw = d
    pw = pool_scale.shape[1]
    n_ctx = batch * seq
    m = n_ctx + dec_batch * dec_seq
    dims = dict(n_ctx=n_ctx, seq=seq, dec_seq=dec_seq, qw=qw, kvw=kvw, pw=pw, hd=hd)
    n_exp = router_w.shape[2]
    gw = w_in.shape[2] - qw - 2 * kvw - pw
    assert hd == LANES and d % LANES == 0 and gw == 2 * d

    def grp_for(tile_rows):
        return lambda i: jnp.maximum((i * tile_rows - n_ctx) // dec_seq + 1, 0)

    n_grp = 1 + dec_batch
    rows = -(-n_grp // SUBLANES) * SUBLANES
    cs = jnp.zeros((rows, d), F32).at[0].set(c_ctx).at[1:n_grp].set(c)
    mod = _modulation(cs, w_mod, b_mod)[:, :n_grp]
    mod = mod.reshape(depth, n_grp, N_MOD, 1, d).transpose(0, 2, 1, 3, 4)

    tabs = _rope_tables(dec_seq, hd)
    bands = _pool_bands(min(POOL_BLOCK, seq))
    w_in_b = w_in.astype(BF16)
    w_out_b = w_out.astype(BF16)
    moe_w_down_b = moe_w_down.astype(BF16)

    x = jnp.concatenate([x_prompt.reshape(n_ctx, d), x_sample.reshape(-1, d)], axis=0)
    tm_rn = _tile(m, 512, SUBLANES)
    tm_op = _tile(m, 256, SUBLANES)
    _, h = _resid_norm(x, None, None, norm1_g[0], mod[0, 1], mod[0, 0], grp_for(tm_rn), final=False)

    new_k, new_v = [], []
    for l in range(depth):
        sh1, sc1, g1, sh2, sc2, g2 = (mod[l, k] for k in range(N_MOD))
        q = _proj(h, w_in_b, l, 0, qw, dims, mode="q", gain=q_norm_g[l], tabs=tabs)
        k, kc = _proj(h, w_in_b, l, qw, kvw, dims, mode="k", gain=k_norm_g[l], tabs=tabs, with_cache=True)
        v, vc = _proj(h, w_in_b, l, qw + kvw, kvw, dims, mode="v", with_cache=True)
        u = _proj(h, w_in_b, l, qw + 2 * kvw, pw, dims, mode="u")
        gates = _proj(h, w_in_b, l, qw + 2 * kvw + pw, gw, dims, mode="gate")
        new_k.append(kc.reshape(batch, seq, n_kv, hd))
        new_v.append(vc.reshape(batch, seq, n_kv, hd))
        attn = _attention(q, k, v, cache_k, cache_v, l, dims)
        pool = _pool(u, bands, w_pool, pool_scale[l:l + 1], l, dims)
        last = l == depth - 1
        if last:
            n_next, sc_next, sh_next = final_norm_g, None, None
        else:
            n_next, sc_next, sh_next = norm1_g[l + 1], mod[l + 1, 1], mod[l + 1, 0]
        j = l // 2
        if l % 2 == 0:
            x, h2 = _outproj(attn, pool, gates, x, w_out_b[l], g1, norm2_g[l], sc2, sh2, grp_for(tm_op), None)
            tm = _tile(m, 1024, SUBLANES)
            nt = jnp.full((1,), m // tm, jnp.int32)
            o = _swiglu(h2, jnp.zeros((m // tm,), jnp.int32), nt, tm,
                        dense_w_gate[:, None], dense_w_up[:, None], dense_w_down[:, None], j, token_major=False)
            x, h = _resid_norm(x, o, g2, n_next, sc_next, sh_next, grp_for(tm_rn), final=last)
        else:
            x, h2_tok, route = _outproj(attn, pool, gates, x, w_out_b[l], g1, norm2_g[l], sc2, sh2,
                                        grp_for(tm_op), router_w[j])
            tm = _tile(m, 512, SUBLANES)
            pos, row_token, tile_expert, n_used = _route_plan(route, n_exp, tm)
            xs = _gather_rows(h2_tok, row_token, d)
            o_tok = _swiglu(xs, tile_expert, n_used, tm, moe_w_gate, moe_w_up, moe_w_down_b, j, token_major=True)
            x, h = _combine(o_tok, pos, route, x, g2, n_next, sc_next, sh_next, grp_for(tm_op), final=last)

    y_prompt = h[:n_ctx].reshape(batch, seq, d)
    y_sample = h[n_ctx:].reshape(dec_batch, dec_seq, d)
    return (y_prompt, y_sample, jnp.stack(new_k, axis=1), jnp.stack(new_v, axis=1))
```

```python
import functools
import math

import jax
import jax.numpy as jnp
from jax import lax
from jax.experimental import pallas as pl
from jax.experimental.pallas import tpu as pltpu

GRID_W = 64
ROPE_THETA = 10000.0
POOL_WINDOWS = (2, 4, 8, 16)
TOP_K = 2
N_MOD = 6
EPS = 1e-6

LANES = 128
SUBLANES = 8
VMEM_BUDGET = 60000 * 1024
POOL_BLOCK = 256
ROW_CHUNK = 256
DMA_UNROLL = 8

F32 = jnp.float32
BF16 = jnp.bfloat16


def _params(semantics, vmem_bytes):
    limit = int(min(VMEM_BUDGET, vmem_bytes * 5 // 4 + (4 << 20)))
    return pltpu.CompilerParams(dimension_semantics=semantics, vmem_limit_bytes=limit)


def _tile(n, pref, align):
    t = min(n, pref)
    t -= t % align
    while t > align and n % t:
        t -= align
    assert t >= align and n % t == 0, (n, pref, align)
    return t


def _rms(x):
    return x * lax.rsqrt(jnp.mean(x * x, axis=-1, keepdims=True) + EPS)


def _sigmoid(x):
    return 1.0 / (1.0 + jnp.exp(-x))


def _mod_kernel(c_ref, w_ref, b_ref, o_ref):
    c = c_ref[...]
    s = c * _sigmoid(c)
    o_ref[...] = jnp.dot(s, w_ref[...], preferred_element_type=F32,
                         precision=lax.Precision.HIGHEST) + b_ref[...]


def _modulation(cs, w_mod, b_mod):
    depth, d, n = w_mod.shape
    rows = cs.shape[0]
    tn = _tile(n, 1024, LANES)
    return pl.pallas_call(
        _mod_kernel,
        grid=(depth, n // tn),
        in_specs=[pl.BlockSpec((rows, d), lambda l, j: (0, 0)),
                  pl.BlockSpec((None, d, tn), lambda l, j: (l, 0, j)),
                  pl.BlockSpec((None, 1, tn), lambda l, j: (l, 0, j))],
        out_specs=pl.BlockSpec((None, rows, tn), lambda l, j: (l, 0, j)),
        out_shape=jax.ShapeDtypeStruct((depth, rows, n), F32),
        compiler_params=_params(("arbitrary", "arbitrary"), 2 * d * tn * 4 + 4 * rows * (d + tn) * 4),
        name="modulation",
    )(cs, w_mod, b_mod.reshape(depth, 1, n))


def _resid_norm_kernel(*refs, has_res, final):
    it = iter(refs)
    x_ref = next(it)
    if has_res:
        o_ref, g_ref = next(it), next(it)
    n_ref = next(it)
    if not final:
        sc_ref, sh_ref = next(it), next(it)
    outs = list(it)
    x = x_ref[...]
    if has_res:
        x = x + g_ref[...] * o_ref[...]
        outs[0][...] = x
        outs = outs[1:]
    y = _rms(x) * n_ref[...]
    if final:
        outs[0][...] = y
    else:
        outs[0][...] = (y * (1.0 + sc_ref[...]) + sh_ref[...]).astype(BF16)


def _group_spec(d, grp):
    return pl.BlockSpec((None, 1, d), lambda i: (grp(i), 0, 0))


def _resid_norm(x, o, g, n, sc, sh, grp, *, final):
    m, d = x.shape
    tm = _tile(m, 512, SUBLANES)
    has_res = o is not None
    row = pl.BlockSpec((tm, d), lambda i: (i, 0))
    vec = pl.BlockSpec((1, d), lambda i: (0, 0))
    args, specs = [x], [row]
    if has_res:
        args += [o, g]
        specs += [row, _group_spec(d, grp)]
    args.append(n.reshape(1, d))
    specs.append(vec)
    if not final:
        args += [sc, sh]
        specs += [_group_spec(d, grp)] * 2
    out_shape, out_specs = [], []
    if has_res:
        out_shape.append(jax.ShapeDtypeStruct((m, d), F32))
        out_specs.append(row)
    out_shape.append(jax.ShapeDtypeStruct((m, d), F32 if final else BF16))
    out_specs.append(row)
    res = pl.pallas_call(
        functools.partial(_resid_norm_kernel, has_res=has_res, final=final),
        grid=(m // tm,), in_specs=specs, out_specs=out_specs, out_shape=out_shape,
        compiler_params=_params(("arbitrary",), 8 * tm * d * 4),
        name="resid_norm",
    )(*args)
    return res if has_res else (None, res[0])


def _rope(y, cos, sin):
    half = LANES // 4
    lane = lax.broadcasted_iota(jnp.int32, y.shape, 1)
    partner = jnp.where((lane & (2 * half - 1)) < half,
                        pltpu.roll(y, LANES - half, axis=1), pltpu.roll(y, half, axis=1))
    return y * cos + partner * sin


def _cast_weights_once(w_ref, wb):
    @pl.when(pl.program_id(1) == 0)
    def _():
        wb[...] = w_ref[...].astype(BF16)


def _proj_chunks(h_ref, wb, epilogue):
    w = wb[...]
    rows = h_ref.shape[0]
    rsub = _tile(rows, ROW_CHUNK, SUBLANES)
    for rc in range(rows // rsub):
        sl = slice(rc * rsub, (rc + 1) * rsub)
        epilogue(jnp.dot(h_ref[sl, :], w, preferred_element_type=F32), sl)


def _qk_proj_kernel(h_ref, w_ref, g_ref, cos_ref, sin_ref, o_ref, *rest, nct, hd, mult):
    cache, wb = rest[:-1], rest[-1]
    _cast_weights_once(w_ref, wb)

    def run(is_ctx):
        def epilogue(acc, sl):
            cols = []
            for hh in range(acc.shape[1] // hd):
                y = _rms(acc[:, hh * hd:(hh + 1) * hd]) * g_ref[...]
                if not is_ctx:
                    y = _rope(y, cos_ref[sl, :], sin_ref[sl, :])
                cols.append(y)
            y = jnp.concatenate(cols, axis=1) if len(cols) > 1 else cols[0]
            if is_ctx and cache:
                cache[0][sl, :] = y
            o_ref[sl, :] = (y * mult if mult != 1.0 else y).astype(BF16)
        _proj_chunks(h_ref, wb, epilogue)

    i = pl.program_id(1)
    pl.when(i < nct)(lambda: run(True))
    pl.when(i >= nct)(lambda: run(False))


def _plain_proj_kernel(h_ref, w_ref, o_ref, *rest, nct, gate):
    cache, wb = rest[:-1], rest[-1]
    _cast_weights_once(w_ref, wb)

    def run(to_cache):
        def epilogue(acc, sl):
            o_ref[sl, :] = (_sigmoid(acc) if gate else acc).astype(BF16)
            if to_cache:
                cache[0][sl, :] = acc
        _proj_chunks(h_ref, wb, epilogue)

    if cache:
        i = pl.program_id(1)
        pl.when(i < nct)(lambda: run(True))
        pl.when(i >= nct)(lambda: run(False))
    else:
        run(False)


def _proj(h, w_in, layer, col0, width, dims, *, mode, gain=None, tabs=None, with_cache=False):
    m, d = h.shape
    n_ctx, dec_seq, hd = dims["n_ctx"], dims["dec_seq"], dims["hd"]
    tn = _tile(math.gcd(width, col0), 512, LANES)
    tm = _tile(dec_seq, 1024, SUBLANES)
    assert n_ctx % tm == 0 and col0 % tn == 0 and width % tn == 0
    nct, tps = n_ctx // tm, dec_seq // tm
    args = [h, w_in]
    specs = [pl.BlockSpec((tm, d), lambda j, i: (i, 0)),
             pl.BlockSpec((None, d, tn), lambda j, i: (layer, 0, col0 // tn + j))]
    if mode in ("q", "k"):
        tab = pl.BlockSpec((tm, hd), lambda j, i: (jnp.maximum(i - nct, 0) % tps, 0))
        args += [gain.reshape(1, hd), tabs[0], tabs[1]]
        specs += [pl.BlockSpec((1, hd), lambda j, i: (0, 0)), tab, tab]
        body = functools.partial(_qk_proj_kernel, nct=nct, hd=hd,
                                 mult=float(hd) ** -0.5 if mode == "q" else 1.0)
    else:
        body = functools.partial(_plain_proj_kernel, nct=nct, gate=mode == "gate")
    out_shape = [jax.ShapeDtypeStruct((m, width), BF16)]
    out_specs = [pl.BlockSpec((tm, tn), lambda j, i: (i, j))]
    if with_cache:
        out_shape.append(jax.ShapeDtypeStruct((n_ctx, width), F32))
        out_specs.append(pl.BlockSpec((tm, tn), lambda j, i: (jnp.minimum(i, nct - 1), j)))
    vmem = (2 * (tm * d * 2 + d * tn * 4 + 2 * tm * hd * 4 + tm * tn * 2 + tm * tn * 4)
            + d * tn * 2 + 8 * ROW_CHUNK * tn * 4)
    res = pl.pallas_call(
        body, grid=(width // tn, m // tm), in_specs=specs, out_specs=out_specs, out_shape=out_shape,
        scratch_shapes=[pltpu.VMEM((d, tn), BF16)],
        compiler_params=_params(("arbitrary", "arbitrary"), vmem), name="proj_" + mode,
    )(*args)
    return res if with_cache else res[0]


def _qk(q, k):
    return lax.dot_general(q, k, (((1,), (1,)), ((), ())), preferred_element_type=F32)


def _attn_ctx_kernel(q_ref, k_ref, v_ref, o_ref, *, n_kv, n_g, hd):
    for kv in range(n_kv):
        k = k_ref[:, kv * hd:(kv + 1) * hd]
        v = v_ref[:, kv * hd:(kv + 1) * hd]
        for g in range(n_g):
            c0 = (kv * n_g + g) * hd
            s = _qk(q_ref[:, c0:c0 + hd], k)
            p = jnp.exp(s - jnp.max(s, axis=-1, keepdims=True))
            o = jnp.dot(p.astype(BF16), v, preferred_element_type=F32)
            o_ref[:, c0:c0 + hd] = (o / jnp.sum(p, axis=-1, keepdims=True)).astype(BF16)


def _attn_lat_kernel(q_ref, k_ref, v_ref, ck_ref, cv_ref, o_ref, *, n_g, hd):
    k, v = k_ref[...], v_ref[...]
    ck, cv = ck_ref[...].astype(BF16), cv_ref[...].astype(BF16)
    for g in range(n_g):
        q = q_ref[:, g * hd:(g + 1) * hd]
        s1, s2 = _qk(q, k), _qk(q, ck)
        mx = jnp.maximum(jnp.max(s1, axis=-1, keepdims=True), jnp.max(s2, axis=-1, keepdims=True))
        p1, p2 = jnp.exp(s1 - mx), jnp.exp(s2 - mx)
        den = jnp.sum(p1, axis=-1, keepdims=True) + jnp.sum(p2, axis=-1, keepdims=True)
        o = (jnp.dot(p1.astype(BF16), v, preferred_element_type=F32)
             + jnp.dot(p2.astype(BF16), cv, preferred_element_type=F32))
        o_ref[:, g * hd:(g + 1) * hd] = (o / den).astype(BF16)


def _attention(q, k, v, cache_k, cache_v, layer, dims):
    m, qw = q.shape
    n_ctx, seq, dec_seq, kvw, hd = (dims[key] for key in ("n_ctx", "seq", "dec_seq", "kvw", "hd"))
    n_kv = kvw // hd
    n_g = qw // kvw
    dec_batch, _, past = cache_k.shape[:3]
    ck = cache_k.reshape(dec_batch, cache_k.shape[1], past, kvw)
    cv = cache_v.reshape(dec_batch, cache_v.shape[1], past, kvw)

    ctx = pl.pallas_call(
        functools.partial(_attn_ctx_kernel, n_kv=n_kv, n_g=n_g, hd=hd),
        grid=(n_ctx // seq,),
        in_specs=[pl.BlockSpec((seq, qw), lambda b: (b, 0)),
                  pl.BlockSpec((seq, kvw), lambda b: (b, 0)),
                  pl.BlockSpec((seq, kvw), lambda b: (b, 0))],
        out_specs=pl.BlockSpec((seq, qw), lambda b: (b, 0)),
        out_shape=jax.ShapeDtypeStruct((n_ctx, qw), BF16),
        compiler_params=_params(("arbitrary",), 4 * seq * (qw + kvw) * 2 + 8 * seq * seq * 4),
        name="attn_ctx",
    )(q, k, v)

    assert n_ctx % dec_seq == 0
    tq = _tile(dec_seq, 256, SUBLANES)
    gw = n_g * hd
    qrow = lambda b, kv, t: ((n_ctx + b * dec_seq) // tq + t, kv)
    krow = lambda b, kv, t: (n_ctx // dec_seq + b, kv)
    crow = lambda b, kv, t: (b, layer, 0, kv)
    lat = pl.pallas_call(
        functools.partial(_attn_lat_kernel, n_g=n_g, hd=hd),
        grid=(dec_batch, n_kv, dec_seq // tq),
        in_specs=[pl.BlockSpec((tq, gw), qrow),
                  pl.BlockSpec((dec_seq, hd), krow), pl.BlockSpec((dec_seq, hd), krow),
                  pl.BlockSpec((None, None, past, hd), crow), pl.BlockSpec((None, None, past, hd), crow)],
        out_specs=pl.BlockSpec((tq, gw), lambda b, kv, t: (b * (dec_seq // tq) + t, kv)),
        out_shape=jax.ShapeDtypeStruct((m - n_ctx, qw), BF16),
        compiler_params=_params(("arbitrary",) * 3,
                                4 * tq * gw * 2 + 4 * dec_seq * hd * 2 + 4 * past * hd * 4
                                + 6 * tq * (dec_seq + past) * 4),
        name="attn_lat",
    )(q, k, v, ck, cv)
    return ctx, lat


def _pool_kernel(u_ref, band_ref, w_ref, sc_ref, o_ref, *, nct, seq, dec_seq, windows):
    i, g = pl.program_id(0), pl.program_id(1)
    rows = u_ref.shape[0]
    pb = band_ref.shape[-1]
    w = w_ref[...].astype(BF16)
    half = jnp.zeros((), jnp.int32)
    for gi, win in enumerate(windows):
        half = jnp.where(g == gi, win // 2, half)

    def run(length):
        for b in range(rows // pb):
            p0 = (b * pb) % length
            cur = u_ref[b * pb:(b + 1) * pb, :]
            tot = jnp.dot(band_ref[1], cur, preferred_element_type=F32)
            if p0 > 0:
                tot += jnp.dot(band_ref[0], u_ref[(b - 1) * pb:b * pb, :], preferred_element_type=F32)
            if p0 + pb < length:
                tot += jnp.dot(band_ref[2], u_ref[(b + 1) * pb:(b + 2) * pb, :], preferred_element_type=F32)
            pos = p0 + lax.broadcasted_iota(jnp.int32, (pb, 1), 0)
            cnt = jnp.minimum(pos + half, length) - jnp.maximum(pos - half, 0)
            pooled = tot / cnt.astype(F32) - cur.astype(F32)
            mixed = jnp.dot(pooled.astype(BF16), w, preferred_element_type=F32)
            o_ref[b * pb:(b + 1) * pb, :] = (mixed * sc_ref[...]).astype(BF16)

    @pl.when(i < nct)
    def _():
        run(seq)

    @pl.when(i >= nct)
    def _():
        run(dec_seq)


def _pool_bands(pb):
    t = jnp.arange(pb)[:, None]
    j = jnp.arange(pb)[None, :]
    out = []
    for win in POOL_WINDOWS:
        h = win // 2
        prev = (j - pb - t) >= -h
        cur = ((j - t) >= -h) & ((j - t) <= h - 1)
        nxt = (j + pb - t) <= h - 1
        out.append(jnp.stack([prev, cur, nxt]))
    return jnp.stack(out).astype(BF16)


def _pool(u, bands, w_pool, pool_scale, layer, dims):
    m, pw = u.shape
    n_ctx, seq, dec_seq = dims["n_ctx"], dims["seq"], dims["dec_seq"]
    ng = len(POOL_WINDOWS)
    gd = pw // ng
    pb = bands.shape[-1]
    tr = dec_seq
    assert n_ctx % tr == 0 and tr % seq == 0 and seq % pb == 0 and max(POOL_WINDOWS) // 2 <= pb
    return pl.pallas_call(
        functools.partial(_pool_kernel, nct=n_ctx // tr, seq=seq, dec_seq=dec_seq, windows=POOL_WINDOWS),
        grid=(m // tr, ng),
        in_specs=[pl.BlockSpec((tr, gd), lambda i, g: (i, g)),
                  pl.BlockSpec((None, 3, pb, pb), lambda i, g: (g, 0, 0, 0)),
                  pl.BlockSpec((None, None, gd, gd), lambda i, g: (layer, g, 0, 0)),
                  pl.BlockSpec((1, gd), lambda i, g: (0, g))],
        out_specs=pl.BlockSpec((tr, gd), lambda i, g: (i, g)),
        out_shape=jax.ShapeDtypeStruct((m, pw), BF16),
        compiler_params=_params(("arbitrary", "arbitrary"),
                                4 * tr * gd * 2 + 2 * gd * gd * 4 + 6 * pb * pb * 2 + 8 * pb * gd * 4),
        name="pool_mix",
    )(u, bands, w_pool, pool_scale.reshape(1, pw))


def _outproj_kernel(*refs, n_exp, nct):
    (ac_ref, al_ref, p_ref, ga_ref, gb_ref, x_ref, w_ref, g1_ref, n2_ref, sc_ref, sh_ref) = refs[:11]
    rest = refs[11:]
    attn = jnp.where(pl.program_id(0) < nct, ac_ref[...], al_ref[...])
    merged = (ga_ref[...].astype(F32) * attn.astype(F32)
              + gb_ref[...].astype(F32) * p_ref[...].astype(F32)).astype(BF16)
    mix = jnp.dot(merged, w_ref[...], preferred_element_type=F32)
    x = x_ref[...] + g1_ref[...] * mix
    h2 = _rms(x) * n2_ref[...] * (1.0 + sc_ref[...]) + sh_ref[...]
    if n_exp == 0:
        xo_ref, h_ref = rest
        xo_ref[...] = x
        h_ref[...] = h2.astype(BF16)
        return
    rw_ref, xo_ref, h_ref, rt_ref = rest
    xo_ref[...] = x
    rows, d = h2.shape
    chunks = d // LANES
    for s in range(chunks):
        h_ref[pl.ds(s, rows, stride=chunks), :] = h2[:, s * LANES:(s + 1) * LANES]
    h_hi = h2.astype(BF16)
    h_lo = (h2 - h_hi.astype(F32)).astype(BF16)
    part = (jnp.dot(h_hi, rw_ref[...], preferred_element_type=F32)
            + jnp.dot(h_lo, rw_ref[...], preferred_element_type=F32))
    logits = part[:, :LANES] + part[:, LANES:]
    lane = lax.broadcasted_iota(jnp.int32, logits.shape, 1)
    lane_f = lane.astype(F32)
    neg = jnp.float32(-jnp.inf)
    lg = jnp.where(lane < n_exp, logits, neg)
    m1 = jnp.max(lg, axis=-1, keepdims=True)
    i1 = jnp.min(jnp.where(lg == m1, lane_f, float(LANES)), axis=-1, keepdims=True)
    lg2 = jnp.where(lane_f == i1, neg, lg)
    m2 = jnp.max(lg2, axis=-1, keepdims=True)
    i2 = jnp.min(jnp.where(lg2 == m2, lane_f, float(LANES)), axis=-1, keepdims=True)
    e2 = jnp.exp(m2 - m1)
    den = 1.0 + e2
    rt = jnp.where(lane == 0, i1, 0.0)
    rt = jnp.where(lane == 1, i2, rt)
    rt = jnp.where(lane == 2, 1.0 / den, rt)
    rt_ref[...] = jnp.where(lane == 3, e2 / den, rt)


def _outproj(attn_ctx, attn_lat, pool, gates, x, w_out_b, g1, n2, sc2, sh2, grp, router_w):
    m, d = x.shape
    tm = _tile(math.gcd(m, attn_ctx.shape[0]), 256, SUBLANES)
    nct = attn_ctx.shape[0] // tm
    chunks = d // LANES
    row = pl.BlockSpec((tm, d), lambda i: (i, 0))
    vec = pl.BlockSpec((1, d), lambda i: (0, 0))
    args = [attn_ctx, attn_lat, pool, gates, gates, x, w_out_b, g1, n2.reshape(1, d), sc2, sh2]
    specs = [pl.BlockSpec((tm, d), lambda i: (jnp.minimum(i, nct - 1), 0)),
             pl.BlockSpec((tm, d), lambda i: (jnp.maximum(i - nct, 0), 0)),
             row, row, pl.BlockSpec((tm, d), lambda i: (i, 1)), row,
             pl.BlockSpec((d, d), lambda i: (0, 0)), _group_spec(d, grp), vec,
             _group_spec(d, grp), _group_spec(d, grp)]
    out_shape = [jax.ShapeDtypeStruct((m, d), F32)]
    out_specs = [row]
    n_exp = 0
    if router_w is None:
        out_shape.append(jax.ShapeDtypeStruct((m, d), BF16))
        out_specs.append(row)
    else:
        n_exp = router_w.shape[1]
        assert TOP_K == 2 and 2 * TOP_K <= LANES and n_exp <= LANES
        rw = jnp.pad(router_w, ((0, 0), (0, LANES - n_exp)))
        rw_hi = rw.astype(BF16)
        rw_lo = (rw - rw_hi.astype(F32)).astype(BF16)
        args.append(jnp.concatenate([rw_hi, rw_lo], axis=1))
        specs.append(pl.BlockSpec((d, 2 * LANES), lambda i: (0, 0)))
        out_shape += [jax.ShapeDtypeStruct((m * chunks, LANES), F32), jax.ShapeDtypeStruct((m, LANES), F32)]
        out_specs += [pl.BlockSpec((tm * chunks, LANES), lambda i: (i, 0)),
                      pl.BlockSpec((tm, LANES), lambda i: (i, 0))]
    return pl.pallas_call(
        functools.partial(_outproj_kernel, n_exp=n_exp, nct=nct),
        grid=(m // tm,), in_specs=specs, out_specs=out_specs, out_shape=out_shape,
        compiler_params=_params(("arbitrary",), 2 * tm * d * (4 * 2 + 3 * 4) + 2 * d * d * 2 + 8 * tm * d * 4),
        name="outproj_route" if n_exp else "outproj",
    )(*args)


def _changed(te_ref, i):
    return (i == 0) | (te_ref[i] != te_ref[jnp.maximum(i - 1, 0)])


def _up_kernel(te_ref, nu_ref, x_ref, wg_ref, wu_ref, o_ref, wgb, wub):
    i = pl.program_id(1)

    @pl.when(_changed(te_ref, i))
    def _():
        wgb[...] = wg_ref[...].astype(BF16)
        wub[...] = wu_ref[...].astype(BF16)

    @pl.when(i < nu_ref[0])
    def _():
        x = x_ref[...]
        a = jnp.dot(x, wgb[...], preferred_element_type=F32)
        b = jnp.dot(x, wub[...], preferred_element_type=F32)
        o_ref[...] = (a * _sigmoid(a) * b).astype(BF16)

    @pl.when(i >= nu_ref[0])
    def _():
        o_ref[...] = jnp.zeros(o_ref.shape, BF16)


def _down_kernel(te_ref, nu_ref, a_ref, w_ref, o_ref, wb):
    i = pl.program_id(1)

    @pl.when(_changed(te_ref, i))
    def _():
        wb[...] = w_ref[...].astype(BF16)

    o_ref[...] = jnp.dot(a_ref[...], wb[...], preferred_element_type=F32)


def _down_tok_kernel(te_ref, nu_ref, a_ref, w_ref, o_ref, *, chunks):
    i, j = pl.program_id(0), pl.program_id(1)
    tm = a_ref.shape[0]
    per = w_ref.shape[1] // LANES

    @pl.when(i < nu_ref[0])
    def _():
        acc = jnp.dot(a_ref[...], w_ref[...], preferred_element_type=F32)
        for jj in range(chunks // per):
            @pl.when(j == jj)
            def _():
                for c in range(per):
                    o_ref[pl.ds(jj * per + c, tm, stride=chunks), :] = acc[:, c * LANES:(c + 1) * LANES]

    @pl.when((i >= nu_ref[0]) & (j == 0))
    def _():
        o_ref[...] = jnp.zeros(o_ref.shape, F32)


def _swiglu(xs, tile_expert, n_used, tm, w_gate, w_up, w_down, sl, *, token_major):
    r, d = xs.shape
    f = w_gate.shape[3]
    tf = _tile(f, 512, LANES)
    tn = _tile(d, 512, LANES)
    nt = r // tm
    used = lambda i, nu: jnp.minimum(i, nu[0] - 1)
    act = pl.pallas_call(
        _up_kernel,
        grid_spec=pltpu.PrefetchScalarGridSpec(
            num_scalar_prefetch=2, grid=(f // tf, nt),
            in_specs=[pl.BlockSpec((tm, d), lambda j, i, te, nu: (used(i, nu), 0)),
                      pl.BlockSpec((None, None, d, tf), lambda j, i, te, nu: (sl, te[i], 0, j)),
                      pl.BlockSpec((None, None, d, tf), lambda j, i, te, nu: (sl, te[i], 0, j))],
            out_specs=pl.BlockSpec((tm, tf), lambda j, i, te, nu: (i, j)),
            scratch_shapes=[pltpu.VMEM((d, tf), BF16), pltpu.VMEM((d, tf), BF16)]),
        out_shape=jax.ShapeDtypeStruct((r, f), BF16),
        compiler_params=_params(("arbitrary", "arbitrary"),
                                2 * (tm * d * 2 + 2 * d * tf * 4 + tm * tf * 2) + 2 * d * tf * 2 + 4 * tm * tf * 4),
        name="swiglu_up",
    )(tile_expert, n_used, xs, w_gate, w_up)
    if not token_major:
        return pl.pallas_call(
            _down_kernel,
            grid_spec=pltpu.PrefetchScalarGridSpec(
                num_scalar_prefetch=2, grid=(d // tn, nt),
                in_specs=[pl.BlockSpec((tm, f), lambda j, i, te, nu: (i, 0)),
                          pl.BlockSpec((None, None, f, tn), lambda j, i, te, nu: (sl, te[i], 0, j))],
                out_specs=pl.BlockSpec((tm, tn), lambda j, i, te, nu: (i, j)),
                scratch_shapes=[pltpu.VMEM((f, tn), BF16)]),
            out_shape=jax.ShapeDtypeStruct((r, d), F32),
            compiler_params=_params(("arbitrary", "arbitrary"),
                                    2 * (tm * f * 2 + f * tn * 4 + tm * tn * 4) + f * tn * 2 + 2 * tm * tn * 4),
            name="swiglu_down",
        )(tile_expert, n_used, act, w_down)
    chunks = d // LANES
    n_j = d // tn
    col = lambda i, j, nu: jnp.where(i < nu[0], j, n_j - 1)
    return pl.pallas_call(
        functools.partial(_down_tok_kernel, chunks=chunks),
        grid_spec=pltpu.PrefetchScalarGridSpec(
            num_scalar_prefetch=2, grid=(nt, n_j),
            in_specs=[pl.BlockSpec((tm, f), lambda i, j, te, nu: (used(i, nu), 0)),
                      pl.BlockSpec((None, None, f, tn), lambda i, j, te, nu: (sl, te[i], 0, col(i, j, nu)))],
            out_specs=pl.BlockSpec((tm * chunks, LANES), lambda i, j, te, nu: (i, 0))),
        out_shape=jax.ShapeDtypeStruct((r * chunks, LANES), F32),
        compiler_params=_params(("arbitrary", "arbitrary"),
                                2 * (tm * f * 2 + f * tn * 2 + tm * d * 4) + 4 * tm * tn * 4),
        name="swiglu_down_tok",
    )(tile_expert, n_used, act, w_down)


def _row_copy(src_hbm, src_row, dst, dst_row, sem, chunks):
    return pltpu.make_async_copy(
        src_hbm.at[pl.ds(pl.multiple_of(src_row * chunks, chunks), chunks), :],
        dst.at[pl.ds(pl.multiple_of(dst_row * chunks, chunks), chunks), :], sem)


def _start_rows(idx_ref, base, stride, n, src_hbm, dst, sem, chunks):
    def body(o, c):
        for u in range(DMA_UNROLL):
            r = o * DMA_UNROLL + u
            _row_copy(src_hbm, idx_ref[base + stride * r], dst, r, sem, chunks).start(priority=u % 2)
        return c
    lax.fori_loop(0, n // DMA_UNROLL, body, 0)


def _wait_rows(n, src_hbm, dst, sem, chunks):
    def body(o, c):
        for u in range(DMA_UNROLL):
            _row_copy(src_hbm, 0, dst, o * DMA_UNROLL + u, sem, chunks).wait()
        return c
    lax.fori_loop(0, n // DMA_UNROLL, body, 0)


def _gather_kernel(rt_ref, h_hbm, o_ref, buf, sems, *, chunks):
    i, n = pl.program_id(0), pl.num_programs(0)
    tg = o_ref.shape[0]
    slot = i % 2

    @pl.when(i == 0)
    def _():
        _start_rows(rt_ref, 0, 1, tg, h_hbm, buf.at[0], sems.at[0], chunks)

    @pl.when(i + 1 < n)
    def _():
        _start_rows(rt_ref, (i + 1) * tg, 1, tg, h_hbm, buf.at[1 - slot], sems.at[1 - slot], chunks)

    cur = buf.at[slot]
    _wait_rows(tg, h_hbm, cur, sems.at[slot], chunks)
    for s in range(chunks):
        o_ref[:, s * LANES:(s + 1) * LANES] = cur[pl.ds(s, tg, stride=chunks), :].astype(BF16)


def _gather_rows(h_tok, row_token, d):
    chunks = d // LANES
    r = row_token.shape[0]
    tg = _tile(r, 256, SUBLANES * DMA_UNROLL)
    return pl.pallas_call(
        functools.partial(_gather_kernel, chunks=chunks),
        grid_spec=pltpu.PrefetchScalarGridSpec(
            num_scalar_prefetch=1, grid=(r // tg,),
            in_specs=[pl.BlockSpec(memory_space=pl.ANY)],
            out_specs=pl.BlockSpec((tg, d), lambda i, rt: (i, 0)),
            scratch_shapes=[pltpu.VMEM((2, tg * chunks, LANES), F32), pltpu.SemaphoreType.DMA((2,))]),
        out_shape=jax.ShapeDtypeStruct((r, d), BF16),
        compiler_params=_params(("arbitrary",), 2 * tg * d * 4 + 2 * tg * d * 2 + 2 * tg * d * 4),
        name="moe_gather",
    )(row_token, h_tok)


def _combine_kernel(*refs, chunks, final):
    pos_ref, o_hbm, rt_ref, x_ref, g_ref, n_ref = refs[:6]
    rest = list(refs[6:])
    if not final:
        sc_ref, sh_ref = rest[:2]
        rest = rest[2:]
    xo_ref, y_ref, bufs, sems = rest
    i, n = pl.program_id(0), pl.num_programs(0)
    tc = x_ref.shape[0]
    slot = i % 2

    def start(step, sl):
        for k in range(TOP_K):
            _start_rows(pos_ref, TOP_K * step * tc + k, TOP_K, tc, o_hbm, bufs.at[sl, k], sems.at[sl, k], chunks)

    @pl.when(i == 0)
    def _():
        start(0, 0)

    @pl.when(i + 1 < n)
    def _():
        start(i + 1, 1 - slot)

    for k in range(TOP_K):
        _wait_rows(tc, o_hbm, bufs.at[slot, k], sems.at[slot, k], chunks)
    w0 = rt_ref[:, TOP_K:TOP_K + 1]
    w1 = rt_ref[:, TOP_K + 1:TOP_K + 2]
    b0, b1 = bufs.at[slot, 0], bufs.at[slot, 1]
    cols = [w0 * b0[pl.ds(s, tc, stride=chunks), :] + w1 * b1[pl.ds(s, tc, stride=chunks), :]
            for s in range(chunks)]
    x = x_ref[...] + g_ref[...] * jnp.concatenate(cols, axis=1)
    xo_ref[...] = x
    y = _rms(x) * n_ref[...]
    if final:
        y_ref[...] = y
    else:
        y_ref[...] = (y * (1.0 + sc_ref[...]) + sh_ref[...]).astype(BF16)


def _combine(o_tok, pos, route, x, g, n, sc, sh, grp, *, final):
    m, d = x.shape
    chunks = d // LANES
    tc = _tile(m, 256, SUBLANES * DMA_UNROLL)
    row = pl.BlockSpec((tc, d), lambda i, p: (i, 0))
    gspec = pl.BlockSpec((None, 1, d), lambda i, p: (grp(i), 0, 0))
    args = [pos, o_tok, route, x, g, n.reshape(1, d)]
    specs = [pl.BlockSpec(memory_space=pl.ANY), pl.BlockSpec((tc, LANES), lambda i, p: (i, 0)), row, gspec,
             pl.BlockSpec((1, d), lambda i, p: (0, 0))]
    if not final:
        args += [sc, sh]
        specs += [gspec, gspec]
    return pl.pallas_call(
        functools.partial(_combine_kernel, chunks=chunks, final=final),
        grid_spec=pltpu.PrefetchScalarGridSpec(
            num_scalar_prefetch=1, grid=(m // tc,), in_specs=specs,
            out_specs=[row, row],
            scratch_shapes=[pltpu.VMEM((2, TOP_K, tc * chunks, LANES), F32),
                            pltpu.SemaphoreType.DMA((2, TOP_K))]),
        out_shape=[jax.ShapeDtypeStruct((m, d), F32), jax.ShapeDtypeStruct((m, d), F32 if final else BF16)],
        compiler_params=_params(("arbitrary",), 2 * TOP_K * tc * d * 4 + 6 * tc * d * 4 + 6 * tc * d * 4),
        name="moe_combine",
    )(*args)


def _route_plan(route, n_exp, tm):
    m = route.shape[0]
    flat_e = route[:, :TOP_K].astype(jnp.int32).reshape(-1)
    onehot = (flat_e[:, None] == jnp.arange(n_exp)[None, :]).astype(jnp.int32)
    csum = jnp.cumsum(onehot, axis=0)
    rank = jnp.take_along_axis(csum, flat_e[:, None], axis=1)[:, 0] - 1
    counts = csum[-1]
    padded = (counts + tm - 1) // tm * tm
    starts = jnp.cumsum(padded) - padded
    pos = starts[flat_e] + rank
    n_rows = TOP_K * m + n_exp * tm
    row_token = jnp.zeros((n_rows,), jnp.int32).at[pos].set(jnp.arange(TOP_K * m, dtype=jnp.int32) // TOP_K)
    tiles = jnp.arange(n_rows // tm, dtype=jnp.int32)
    tile_expert = jnp.sum(tiles[:, None] * tm >= starts[None, :], axis=1).astype(jnp.int32) - 1
    n_used = (jnp.sum(padded) // tm).astype(jnp.int32).reshape(1)
    return pos.astype(jnp.int32), row_token, tile_expert, n_used


def _rope_tables(dec_seq, hd):
    half = hd // 4
    rows = dec_seq // GRID_W
    row_pos = jnp.broadcast_to(jnp.arange(rows)[:, None], (rows, GRID_W)).reshape(-1).astype(F32)
    col_pos = jnp.broadcast_to(jnp.arange(GRID_W)[None, :], (rows, GRID_W)).reshape(-1).astype(F32)
    inv_freq = ROPE_THETA ** (-jnp.arange(0, 2 * half, 2, dtype=F32) / (2 * half))
    ang_r = row_pos[:, None] * inv_freq[None, :]
    ang_c = col_pos[:, None] * inv_freq[None, :]
    cos = jnp.concatenate([jnp.cos(ang_r)] * 2 + [jnp.cos(ang_c)] * 2, axis=1)
    sin = jnp.concatenate([-jnp.sin(ang_r), jnp.sin(ang_r), -jnp.sin(ang_c), jnp.sin(ang_c)], axis=1)
    return cos, sin


def kernel(x_prompt, x_sample, c, cache_k, cache_v, c_ctx, w_mod, b_mod, norm1_g, norm2_g, w_in,
           q_norm_g, k_norm_g, w_pool, pool_scale, w_out, dense_w_gate, dense_w_up, dense_w_down,
           router_w, moe_w_gate, moe_w_up, moe_w_down, final_norm_g):
    batch, seq, d = x_prompt.shape
    dec_batch, dec_seq, _ = x_sample.shape
    depth = w_mod.shape[0]
    n_kv, hd = cache_k.shape[3], cache_k.shape[4]
    kvw = n_kv * hd
    qw = d
    pw = pool_scale.shape[1]
    n_ctx = batch * seq
    m = n_ctx + dec_batch * dec_seq
    dims = dict(n_ctx=n_ctx, seq=seq, dec_seq=dec_seq, qw=qw, kvw=kvw, pw=pw, hd=hd)
    n_exp = router_w.shape[2]
    gw = w_in.shape[2] - qw - 2 * kvw - pw
    assert hd == LANES and d % LANES == 0 and gw == 2 * d

    def grp_for(tile_rows):
        return lambda i: jnp.maximum((i * tile_rows - n_ctx) // dec_seq + 1, 0)

    n_grp = 1 + dec_batch
    rows = -(-n_grp // SUBLANES) * SUBLANES
    cs = jnp.zeros((rows, d), F32).at[0].set(c_ctx).at[1:n_grp].set(c)
    mod = _modulation(cs, w_mod, b_mod)[:, :n_grp]
    mod = mod.reshape(depth, n_grp, N_MOD, 1, d).transpose(0, 2, 1, 3, 4)

    tabs = _rope_tables(dec_seq, hd)
    bands = _pool_bands(min(POOL_BLOCK, seq))
    w_out_b = w_out.astype(BF16)
    moe_w_down_b = moe_w_down.astype(BF16)

    x = jnp.concatenate([x_prompt.reshape(n_ctx, d), x_sample.reshape(-1, d)], axis=0)
    tm_rn = _tile(m, 512, SUBLANES)
    tm_op = _tile(math.gcd(m, n_ctx), 256, SUBLANES)
    _, h = _resid_norm(x, None, None, norm1_g[0], mod[0, 1], mod[0, 0], grp_for(tm_rn), final=False)

    new_k, new_v = [], []
    for l in range(depth):
        sh1, sc1, g1, sh2, sc2, g2 = (mod[l, k] for k in range(N_MOD))
        q = _proj(h, w_in, l, 0, qw, dims, mode="q", gain=q_norm_g[l], tabs=tabs)
        k, kc = _proj(h, w_in, l, qw, kvw, dims, mode="k", gain=k_norm_g[l], tabs=tabs, with_cache=True)
        v, vc = _proj(h, w_in, l, qw + kvw, kvw, dims, mode="v", with_cache=True)
        u = _proj(h, w_in, l, qw + 2 * kvw, pw, dims, mode="u")
        gates = _proj(h, w_in, l, qw + 2 * kvw + pw, gw, dims, mode="gate")
        new_k.append(kc.reshape(batch, seq, n_kv, hd))
        new_v.append(vc.reshape(batch, seq, n_kv, hd))
        attn_ctx, attn_lat = _attention(q, k, v, cache_k, cache_v, l, dims)
        pool = _pool(u, bands, w_pool, pool_scale[l:l + 1], l, dims)
        last = l == depth - 1
        if last:
            n_next, sc_next, sh_next = final_norm_g, None, None
        else:
            n_next, sc_next, sh_next = norm1_g[l + 1], mod[l + 1, 1], mod[l + 1, 0]
        j = l // 2
        if l % 2 == 0:
            x, h2 = _outproj(attn_ctx, attn_lat, pool, gates, x, w_out_b[l], g1, norm2_g[l], sc2, sh2, grp_for(tm_op), None)
            tm = _tile(m, 1024, SUBLANES)
            nt = jnp.full((1,), m // tm, jnp.int32)
            o = _swiglu(h2, jnp.zeros((m // tm,), jnp.int32), nt, tm,
                        dense_w_gate[:, None], dense_w_up[:, None], dense_w_down[:, None], j, token_major=False)
            x, h = _resid_norm(x, o, g2, n_next, sc_next, sh_next, grp_for(tm_rn), final=last)
        else:
            x, h2_tok, route = _outproj(attn_ctx, attn_lat, pool, gates, x, w_out_b[l], g1, norm2_g[l], sc2, sh2,
                                        grp_for(tm_op), router_w[j])
            tm = _tile(m, 512, SUBLANES)
            pos, row_token, tile_expert, n_used = _route_plan(route, n_exp, tm)
            xs = _gather_rows(h2_tok, row_token, d)
            o_tok = _swiglu(xs, tile_expert, n_used, tm, moe_w_gate, moe_w_up, moe_w_down_b, j, token_major=True)
            x, h = _combine(o_tok, pos, route, x, g2, n_next, sc_next, sh_next, grp_for(tm_op), final=last)

    y_prompt = h[:n_ctx].reshape(batch, seq, d)
    y_sample = h[n_ctx:].reshape(dec_batch, dec_seq, d)
    return (y_prompt, y_sample, jnp.stack(new_k, axis=1), jnp.stack(new_v, axis=1))
```

```python
import functools
import math

import jax
import jax.numpy as jnp
from jax import lax
from jax.experimental import pallas as pl
from jax.experimental.pallas import tpu as pltpu

GRID_W = 64
ROPE_THETA = 10000.0
POOL_WINDOWS = (2, 4, 8, 16)
TOP_K = 2
N_MOD = 6
EPS = 1e-6

LANES = 128
SUBLANES = 8
VMEM_BUDGET = 60000 * 1024
POOL_BLOCK = 256
ROW_CHUNK = 256
DMA_UNROLL = 8
QK_SCALE_LOG2E = math.log2(math.e)
ATTN_ROWS = 512
ATTN_CHAIN_ROWS = 512

F32 = jnp.float32
BF16 = jnp.bfloat16


def _params(semantics, vmem_bytes):
    limit = int(min(VMEM_BUDGET, vmem_bytes * 5 // 4 + (4 << 20)))
    return pltpu.CompilerParams(dimension_semantics=semantics, vmem_limit_bytes=limit)


def _tile(n, pref, align):
    t = min(n, pref)
    t -= t % align
    while t > align and n % t:
        t -= align
    assert t >= align and n % t == 0, (n, pref, align)
    return t


def _rms(x):
    return x * lax.rsqrt(jnp.mean(x * x, axis=-1, keepdims=True) + EPS)


def _sigmoid(x):
    return 1.0 / (1.0 + jnp.exp(-x))


def _mod_kernel(c_ref, w_ref, b_ref, o_ref):
    c = c_ref[...]
    s = c * _sigmoid(c)
    o_ref[...] = jnp.dot(s, w_ref[...], preferred_element_type=F32,
                         precision=lax.Precision.HIGHEST) + b_ref[...]


def _modulation(cs, w_mod, b_mod):
    depth, d, n = w_mod.shape
    rows = cs.shape[0]
    tn = _tile(n, 1024, LANES)
    return pl.pallas_call(
        _mod_kernel,
        grid=(depth, n // tn),
        in_specs=[pl.BlockSpec((rows, d), lambda l, j: (0, 0)),
                  pl.BlockSpec((None, d, tn), lambda l, j: (l, 0, j)),
                  pl.BlockSpec((None, 1, tn), lambda l, j: (l, 0, j))],
        out_specs=pl.BlockSpec((None, rows, tn), lambda l, j: (l, 0, j)),
        out_shape=jax.ShapeDtypeStruct((depth, rows, n), F32),
        compiler_params=_params(("arbitrary", "arbitrary"), 2 * d * tn * 4 + 4 * rows * (d + tn) * 4),
        name="modulation",
    )(cs, w_mod, b_mod.reshape(depth, 1, n))


def _resid_norm_kernel(*refs, has_res, final):
    it = iter(refs)
    x_ref = next(it)
    if has_res:
        o_ref, g_ref = next(it), next(it)
    n_ref = next(it)
    if not final:
        sc_ref, sh_ref = next(it), next(it)
    outs = list(it)
    x = x_ref[...]
    if has_res:
        x = x + g_ref[...] * o_ref[...]
        outs[0][...] = x
        outs = outs[1:]
    y = _rms(x) * n_ref[...]
    if final:
        outs[0][...] = y
    else:
        outs[0][...] = (y * (1.0 + sc_ref[...]) + sh_ref[...]).astype(BF16)


def _group_spec(d, grp):
    return pl.BlockSpec((None, 1, d), lambda i: (grp(i), 0, 0))


def _resid_norm(x, o, g, n, sc, sh, grp, *, final):
    m, d = x.shape
    tm = _tile(m, 512, SUBLANES)
    has_res = o is not None
    row = pl.BlockSpec((tm, d), lambda i: (i, 0))
    vec = pl.BlockSpec((1, d), lambda i: (0, 0))
    args, specs = [x], [row]
    if has_res:
        args += [o, g]
        specs += [row, _group_spec(d, grp)]
    args.append(n.reshape(1, d))
    specs.append(vec)
    if not final:
        args += [sc, sh]
        specs += [_group_spec(d, grp)] * 2
    out_shape, out_specs = [], []
    if has_res:
        out_shape.append(jax.ShapeDtypeStruct((m, d), F32))
        out_specs.append(row)
    out_shape.append(jax.ShapeDtypeStruct((m, d), F32 if final else BF16))
    out_specs.append(row)
    res = pl.pallas_call(
        functools.partial(_resid_norm_kernel, has_res=has_res, final=final),
        grid=(m // tm,), in_specs=specs, out_specs=out_specs, out_shape=out_shape,
        compiler_params=_params(("arbitrary",), 8 * tm * d * 4),
        name="resid_norm",
    )(*args)
    return res if has_res else (None, res[0])


def _rope(y, cos, sin):
    half = LANES // 4
    lane = lax.broadcasted_iota(jnp.int32, y.shape, 1)
    partner = jnp.where((lane & (2 * half - 1)) < half,
                        pltpu.roll(y, LANES - half, axis=1), pltpu.roll(y, half, axis=1))
    return y * cos + partner * sin


def _cast_weights_once(w_ref, wb):
    @pl.when(pl.program_id(1) == 0)
    def _():
        wb[...] = w_ref[...].astype(BF16)


def _proj_chunks(h_ref, wb, epilogue):
    w = wb[...]
    rows = h_ref.shape[0]
    rsub = _tile(rows, ROW_CHUNK, SUBLANES)
    for rc in range(rows // rsub):
        sl = slice(rc * rsub, (rc + 1) * rsub)
        epilogue(jnp.dot(h_ref[sl, :], w, preferred_element_type=F32), sl)


def _qk_proj_kernel(h_ref, w_ref, g_ref, cos_ref, sin_ref, o_ref, *rest, nct, hd, mult):
    cache, wb = rest[:-1], rest[-1]
    _cast_weights_once(w_ref, wb)

    def run(is_ctx):
        def epilogue(acc, sl):
            cols = []
            for hh in range(acc.shape[1] // hd):
                y = _rms(acc[:, hh * hd:(hh + 1) * hd]) * g_ref[...]
                if not is_ctx:
                    y = _rope(y, cos_ref[sl, :], sin_ref[sl, :])
                cols.append(y)
            y = jnp.concatenate(cols, axis=1) if len(cols) > 1 else cols[0]
            if is_ctx and cache:
                cache[0][sl, :] = y
            o_ref[sl, :] = (y * mult if mult != 1.0 else y).astype(BF16)
        _proj_chunks(h_ref, wb, epilogue)

    i = pl.program_id(1)
    pl.when(i < nct)(lambda: run(True))
    pl.when(i >= nct)(lambda: run(False))


def _plain_proj_kernel(h_ref, w_ref, o_ref, *rest, nct, gate):
    cache, wb = rest[:-1], rest[-1]
    _cast_weights_once(w_ref, wb)

    def run(to_cache):
        def epilogue(acc, sl):
            o_ref[sl, :] = (_sigmoid(acc) if gate else acc).astype(BF16)
            if to_cache:
                cache[0][sl, :] = acc
        _proj_chunks(h_ref, wb, epilogue)

    if cache:
        i = pl.program_id(1)
        pl.when(i < nct)(lambda: run(True))
        pl.when(i >= nct)(lambda: run(False))
    else:
        run(False)


def _proj(h, w_in, layer, col0, width, dims, *, mode, gain=None, tabs=None, with_cache=False):
    m, d = h.shape
    n_ctx, dec_seq, hd = dims["n_ctx"], dims["dec_seq"], dims["hd"]
    tn = _tile(math.gcd(width, col0), 512, LANES)
    tm = _tile(dec_seq, 1024, SUBLANES)
    assert n_ctx % tm == 0 and col0 % tn == 0 and width % tn == 0
    nct, tps = n_ctx // tm, dec_seq // tm
    args = [h, w_in]
    specs = [pl.BlockSpec((tm, d), lambda j, i: (i, 0)),
             pl.BlockSpec((None, d, tn), lambda j, i: (layer, 0, col0 // tn + j))]
    if mode in ("q", "k"):
        tab = pl.BlockSpec((tm, hd), lambda j, i: (jnp.maximum(i - nct, 0) % tps, 0))
        args += [gain.reshape(1, hd), tabs[0], tabs[1]]
        specs += [pl.BlockSpec((1, hd), lambda j, i: (0, 0)), tab, tab]
        body = functools.partial(_qk_proj_kernel, nct=nct, hd=hd,
                                 mult=QK_SCALE_LOG2E * float(hd) ** -0.5 if mode == "q" else 1.0)
    else:
        body = functools.partial(_plain_proj_kernel, nct=nct, gate=mode == "gate")
    out_shape = [jax.ShapeDtypeStruct((m, width), BF16)]
    out_specs = [pl.BlockSpec((tm, tn), lambda j, i: (i, j))]
    if with_cache:
        out_shape.append(jax.ShapeDtypeStruct((n_ctx, width), F32))
        out_specs.append(pl.BlockSpec((tm, tn), lambda j, i: (jnp.minimum(i, nct - 1), j)))
    vmem = (2 * (tm * d * 2 + d * tn * 4 + 2 * tm * hd * 4 + tm * tn * 2 + tm * tn * 4)
            + d * tn * 2 + 8 * ROW_CHUNK * tn * 4)
    res = pl.pallas_call(
        body, grid=(width // tn, m // tm), in_specs=specs, out_specs=out_specs, out_shape=out_shape,
        scratch_shapes=[pltpu.VMEM((d, tn), BF16)],
        compiler_params=_params(("arbitrary", "arbitrary"), vmem), name="proj_" + mode,
    )(*args)
    return res if with_cache else res[0]


def _qk(q, k):
    return lax.dot_general(q, k, (((1,), (1,)), ((), ())), preferred_element_type=F32)


def _attn_ctx_kernel(q_ref, k_ref, v_ref, o_ref, *, n_kv, n_g, hd):
    for kv in range(n_kv):
        k = k_ref[:, kv * hd:(kv + 1) * hd]
        v = v_ref[:, kv * hd:(kv + 1) * hd]
        for g in range(n_g):
            c0 = (kv * n_g + g) * hd
            s = _qk(q_ref[:, c0:c0 + hd], k)
            p = jnp.exp2(s - jnp.max(s, axis=-1, keepdims=True))
            o = jnp.dot(p.astype(BF16), v, preferred_element_type=F32)
            o_ref[:, c0:c0 + hd] = (o / jnp.sum(p, axis=-1, keepdims=True)).astype(BF16)


def _attn_lat_kernel(q_ref, k_ref, v_ref, ck_ref, cv_ref, o_ref, *, n_g, hd):
    k, v = k_ref[...], v_ref[...]
    ck, cv = ck_ref[...].astype(BF16), cv_ref[...].astype(BF16)
    tq = q_ref.shape[0]
    rq = _tile(tq, ATTN_CHAIN_ROWS, SUBLANES)
    tdot = lambda a, b: lax.dot_general(a, b, (((0,), (0,)), ((), ())), preferred_element_type=F32)
    for g in range(n_g):
        for r0 in range(0, tq, rq):
            q = q_ref[r0:r0 + rq, g * hd:(g + 1) * hd]
            s1, s2 = _qk(k, q), _qk(ck, q)
            mx = jnp.maximum(jnp.max(s1, axis=0, keepdims=True), jnp.max(s2, axis=0, keepdims=True))
            p1, p2 = jnp.exp2(s1 - mx), jnp.exp2(s2 - mx)
            den = jnp.sum(p1, axis=0, keepdims=True) + jnp.sum(p2, axis=0, keepdims=True)
            o = (tdot(v, p1.astype(BF16)) + tdot(cv, p2.astype(BF16))) / den
            o_ref[r0:r0 + rq, g * hd:(g + 1) * hd] = o.T.astype(BF16)


def _attention(q, k, v, cache_k, cache_v, layer, dims):
    m, qw = q.shape
    n_ctx, seq, dec_seq, kvw, hd = (dims[key] for key in ("n_ctx", "seq", "dec_seq", "kvw", "hd"))
    n_kv = kvw // hd
    n_g = qw // kvw
    dec_batch, _, past = cache_k.shape[:3]
    ck = cache_k.reshape(dec_batch, cache_k.shape[1], past, kvw)
    cv = cache_v.reshape(dec_batch, cache_v.shape[1], past, kvw)

    ctx = pl.pallas_call(
        functools.partial(_attn_ctx_kernel, n_kv=n_kv, n_g=n_g, hd=hd),
        grid=(n_ctx // seq,),
        in_specs=[pl.BlockSpec((seq, qw), lambda b: (b, 0)),
                  pl.BlockSpec((seq, kvw), lambda b: (b, 0)),
                  pl.BlockSpec((seq, kvw), lambda b: (b, 0))],
        out_specs=pl.BlockSpec((seq, qw), lambda b: (b, 0)),
        out_shape=jax.ShapeDtypeStruct((n_ctx, qw), BF16),
        compiler_params=_params(("arbitrary",), 4 * seq * (qw + kvw) * 2 + 8 * seq * seq * 4),
        name="attn_ctx",
    )(q, k, v)

    assert n_ctx % dec_seq == 0
    tq = _tile(dec_seq, ATTN_ROWS, SUBLANES)
    gw = n_g * hd
    qrow = lambda b, kv, t: ((n_ctx + b * dec_seq) // tq + t, kv)
    krow = lambda b, kv, t: (n_ctx // dec_seq + b, kv)
    crow = lambda b, kv, t: (b, layer, 0, kv)
    lat = pl.pallas_call(
        functools.partial(_attn_lat_kernel, n_g=n_g, hd=hd),
        grid=(dec_batch, n_kv, dec_seq // tq),
        in_specs=[pl.BlockSpec((tq, gw), qrow),
                  pl.BlockSpec((dec_seq, hd), krow), pl.BlockSpec((dec_seq, hd), krow),
                  pl.BlockSpec((None, None, past, hd), crow), pl.BlockSpec((None, None, past, hd), crow)],
        out_specs=pl.BlockSpec((tq, gw), lambda b, kv, t: (b * (dec_seq // tq) + t, kv)),
        out_shape=jax.ShapeDtypeStruct((m - n_ctx, qw), BF16),
        compiler_params=_params(("arbitrary",) * 3,
                                4 * tq * gw * 2 + 4 * dec_seq * hd * 2 + 4 * past * hd * 4
                                + 6 * tq * (dec_seq + past) * 4),
        name="attn_lat",
    )(q, k, v, ck, cv)
    return ctx, lat


def _pool_kernel(u_ref, band_ref, w_ref, sc_ref, o_ref, *, nct, seq, dec_seq, windows):
    i, g = pl.program_id(0), pl.program_id(1)
    rows = u_ref.shape[0]
    pb = band_ref.shape[-1]
    w = w_ref[...].astype(BF16)
    half = jnp.zeros((), jnp.int32)
    for gi, win in enumerate(windows):
        half = jnp.where(g == gi, win // 2, half)

    def run(length):
        for b in range(rows // pb):
            p0 = (b * pb) % length
            cur = u_ref[b * pb:(b + 1) * pb, :]
            tot = jnp.dot(band_ref[1], cur, preferred_element_type=F32)
            if p0 > 0:
                tot += jnp.dot(band_ref[0], u_ref[(b - 1) * pb:b * pb, :], preferred_element_type=F32)
            if p0 + pb < length:
                tot += jnp.dot(band_ref[2], u_ref[(b + 1) * pb:(b + 2) * pb, :], preferred_element_type=F32)
            pos = p0 + lax.broadcasted_iota(jnp.int32, (pb, 1), 0)
            cnt = jnp.minimum(pos + half, length) - jnp.maximum(pos - half, 0)
            pooled = tot / cnt.astype(F32) - cur.astype(F32)
            mixed = jnp.dot(pooled.astype(BF16), w, preferred_element_type=F32)
            o_ref[b * pb:(b + 1) * pb, :] = (mixed * sc_ref[...]).astype(BF16)

    @pl.when(i < nct)
    def _():
        run(seq)

    @pl.when(i >= nct)
    def _():
        run(dec_seq)


def _pool_bands(pb):
    t = jnp.arange(pb)[:, None]
    j = jnp.arange(pb)[None, :]
    out = []
    for win in POOL_WINDOWS:
        h = win // 2
        prev = (j - pb - t) >= -h
        cur = ((j - t) >= -h) & ((j - t) <= h - 1)
        nxt = (j + pb - t) <= h - 1
        out.append(jnp.stack([prev, cur, nxt]))
    return jnp.stack(out).astype(BF16)


def _pool(u, bands, w_pool, pool_scale, layer, dims):
    m, pw = u.shape
    n_ctx, seq, dec_seq = dims["n_ctx"], dims["seq"], dims["dec_seq"]
    ng = len(POOL_WINDOWS)
    gd = pw // ng
    pb = bands.shape[-1]
    tr = dec_seq
    assert n_ctx % tr == 0 and tr % seq == 0 and seq % pb == 0 and max(POOL_WINDOWS) // 2 <= pb
    return pl.pallas_call(
        functools.partial(_pool_kernel, nct=n_ctx // tr, seq=seq, dec_seq=dec_seq, windows=POOL_WINDOWS),
        grid=(m // tr, ng),
        in_specs=[pl.BlockSpec((tr, gd), lambda i, g: (i, g)),
                  pl.BlockSpec((None, 3, pb, pb), lambda i, g: (g, 0, 0, 0)),
                  pl.BlockSpec((None, None, gd, gd), lambda i, g: (layer, g, 0, 0)),
                  pl.BlockSpec((1, gd), lambda i, g: (0, g))],
        out_specs=pl.BlockSpec((tr, gd), lambda i, g: (i, g)),
        out_shape=jax.ShapeDtypeStruct((m, pw), BF16),
        compiler_params=_params(("arbitrary", "arbitrary"),
                                4 * tr * gd * 2 + 2 * gd * gd * 4 + 6 * pb * pb * 2 + 8 * pb * gd * 4),
        name="pool_mix",
    )(u, bands, w_pool, pool_scale.reshape(1, pw))


def _outproj_kernel(*refs, n_exp, nct):
    (ac_ref, al_ref, p_ref, ga_ref, gb_ref, x_ref, w_ref, g1_ref, n2_ref, sc_ref, sh_ref) = refs[:11]
    rest = refs[11:]
    attn = jnp.where(pl.program_id(0) < nct, ac_ref[...], al_ref[...])
    merged = (ga_ref[...].astype(F32) * attn.astype(F32)
              + gb_ref[...].astype(F32) * p_ref[...].astype(F32)).astype(BF16)
    mix = jnp.dot(merged, w_ref[...], preferred_element_type=F32)
    x = x_ref[...] + g1_ref[...] * mix
    h2 = _rms(x) * n2_ref[...] * (1.0 + sc_ref[...]) + sh_ref[...]
    if n_exp == 0:
        xo_ref, h_ref = rest
        xo_ref[...] = x
        h_ref[...] = h2.astype(BF16)
        return
    rw_ref, xo_ref, h_ref, rt_ref = rest
    xo_ref[...] = x
    rows, d = h2.shape
    chunks = d // LANES
    for s in range(chunks):
        h_ref[pl.ds(s, rows, stride=chunks), :] = h2[:, s * LANES:(s + 1) * LANES]
    h_hi = h2.astype(BF16)
    h_lo = (h2 - h_hi.astype(F32)).astype(BF16)
    part = (jnp.dot(h_hi, rw_ref[...], preferred_element_type=F32)
            + jnp.dot(h_lo, rw_ref[...], preferred_element_type=F32))
    logits = part[:, :LANES] + part[:, LANES:]
    lane = lax.broadcasted_iota(jnp.int32, logits.shape, 1)
    lane_f = lane.astype(F32)
    neg = jnp.float32(-jnp.inf)
    lg = jnp.where(lane < n_exp, logits, neg)
    m1 = jnp.max(lg, axis=-1, keepdims=True)
    i1 = jnp.min(jnp.where(lg == m1, lane_f, float(LANES)), axis=-1, keepdims=True)
    lg2 = jnp.where(lane_f == i1, neg, lg)
    m2 = jnp.max(lg2, axis=-1, keepdims=True)
    i2 = jnp.min(jnp.where(lg2 == m2, lane_f, float(LANES)), axis=-1, keepdims=True)
    e2 = jnp.exp(m2 - m1)
    den = 1.0 + e2
    rt = jnp.where(lane == 0, i1, 0.0)
    rt = jnp.where(lane == 1, i2, rt)
    rt = jnp.where(lane == 2, 1.0 / den, rt)
    rt_ref[...] = jnp.where(lane == 3, e2 / den, rt)


def _outproj(attn_ctx, attn_lat, pool, gates, x, w_out_b, g1, n2, sc2, sh2, grp, router_w):
    m, d = x.shape
    tm = _tile(math.gcd(m, attn_ctx.shape[0]), 256, SUBLANES)
    nct = attn_ctx.shape[0] // tm
    chunks = d // LANES
    row = pl.BlockSpec((tm, d), lambda i: (i, 0))
    vec = pl.BlockSpec((1, d), lambda i: (0, 0))
    args = [attn_ctx, attn_lat, pool, gates, gates, x, w_out_b, g1, n2.reshape(1, d), sc2, sh2]
    specs = [pl.BlockSpec((tm, d), lambda i: (jnp.minimum(i, nct - 1), 0)),
             pl.BlockSpec((tm, d), lambda i: (jnp.maximum(i - nct, 0), 0)),
             row, row, pl.BlockSpec((tm, d), lambda i: (i, 1)), row,
             pl.BlockSpec((d, d), lambda i: (0, 0)), _group_spec(d, grp), vec,
             _group_spec(d, grp), _group_spec(d, grp)]
    out_shape = [jax.ShapeDtypeStruct((m, d), F32)]
    out_specs = [row]
    n_exp = 0
    if router_w is None:
        out_shape.append(jax.ShapeDtypeStruct((m, d), BF16))
        out_specs.append(row)
    else:
        n_exp = router_w.shape[1]
        assert TOP_K == 2 and 2 * TOP_K <= LANES and n_exp <= LANES
        rw = jnp.pad(router_w, ((0, 0), (0, LANES - n_exp)))
        rw_hi = rw.astype(BF16)
        rw_lo = (rw - rw_hi.astype(F32)).astype(BF16)
        args.append(jnp.concatenate([rw_hi, rw_lo], axis=1))
        specs.append(pl.BlockSpec((d, 2 * LANES), lambda i: (0, 0)))
        out_shape += [jax.ShapeDtypeStruct((m * chunks, LANES), F32), jax.ShapeDtypeStruct((m, LANES), F32)]
        out_specs += [pl.BlockSpec((tm * chunks, LANES), lambda i: (i, 0)),
                      pl.BlockSpec((tm, LANES), lambda i: (i, 0))]
    return pl.pallas_call(
        functools.partial(_outproj_kernel, n_exp=n_exp, nct=nct),
        grid=(m // tm,), in_specs=specs, out_specs=out_specs, out_shape=out_shape,
        compiler_params=_params(("arbitrary",), 2 * tm * d * (4 * 2 + 3 * 4) + 2 * d * d * 2 + 8 * tm * d * 4),
        name="outproj_route" if n_exp else "outproj",
    )(*args)


def _changed(te_ref, i):
    return (i == 0) | (te_ref[i] != te_ref[jnp.maximum(i - 1, 0)])


def _up_kernel(te_ref, nu_ref, x_ref, wg_ref, wu_ref, o_ref, wgb, wub):
    i = pl.program_id(1)

    @pl.when(_changed(te_ref, i))
    def _():
        wgb[...] = wg_ref[...].astype(BF16)
        wub[...] = wu_ref[...].astype(BF16)

    @pl.when(i < nu_ref[0])
    def _():
        x = x_ref[...]
        a = jnp.dot(x, wgb[...], preferred_element_type=F32)
        b = jnp.dot(x, wub[...], preferred_element_type=F32)
        o_ref[...] = (a * _sigmoid(a) * b).astype(BF16)

    @pl.when(i >= nu_ref[0])
    def _():
        o_ref[...] = jnp.zeros(o_ref.shape, BF16)


def _down_kernel(te_ref, nu_ref, a_ref, w_ref, o_ref, wb):
    i = pl.program_id(1)

    @pl.when(_changed(te_ref, i))
    def _():
        wb[...] = w_ref[...].astype(BF16)

    o_ref[...] = jnp.dot(a_ref[...], wb[...], preferred_element_type=F32)


def _down_tok_kernel(te_ref, nu_ref, a_ref, w_ref, o_ref, *, chunks):
    i, j = pl.program_id(0), pl.program_id(1)
    tm = a_ref.shape[0]
    per = w_ref.shape[1] // LANES

    @pl.when(i < nu_ref[0])
    def _():
        acc = jnp.dot(a_ref[...], w_ref[...], preferred_element_type=F32)
        for jj in range(chunks // per):
            @pl.when(j == jj)
            def _():
                for c in range(per):
                    o_ref[pl.ds(jj * per + c, tm, stride=chunks), :] = acc[:, c * LANES:(c + 1) * LANES]

    @pl.when((i >= nu_ref[0]) & (j == 0))
    def _():
        o_ref[...] = jnp.zeros(o_ref.shape, F32)


def _swiglu(xs, tile_expert, n_used, tm, w_gate, w_up, w_down, sl, *, token_major):
    r, d = xs.shape
    f = w_gate.shape[3]
    tf = _tile(f, 512, LANES)
    tn = _tile(d, 512, LANES)
    nt = r // tm
    used = lambda i, nu: jnp.minimum(i, nu[0] - 1)
    act = pl.pallas_call(
        _up_kernel,
        grid_spec=pltpu.PrefetchScalarGridSpec(
            num_scalar_prefetch=2, grid=(f // tf, nt),
            in_specs=[pl.BlockSpec((tm, d), lambda j, i, te, nu: (used(i, nu), 0)),
                      pl.BlockSpec((None, None, d, tf), lambda j, i, te, nu: (sl, te[i], 0, j)),
                      pl.BlockSpec((None, None, d, tf), lambda j, i, te, nu: (sl, te[i], 0, j))],
            out_specs=pl.BlockSpec((tm, tf), lambda j, i, te, nu: (i, j)),
            scratch_shapes=[pltpu.VMEM((d, tf), BF16), pltpu.VMEM((d, tf), BF16)]),
        out_shape=jax.ShapeDtypeStruct((r, f), BF16),
        compiler_params=_params(("arbitrary", "arbitrary"),
                                2 * (tm * d * 2 + 2 * d * tf * 4 + tm * tf * 2) + 2 * d * tf * 2 + 4 * tm * tf * 4),
        name="swiglu_up",
    )(tile_expert, n_used, xs, w_gate, w_up)
    if not token_major:
        return pl.pallas_call(
            _down_kernel,
            grid_spec=pltpu.PrefetchScalarGridSpec(
                num_scalar_prefetch=2, grid=(d // tn, nt),
                in_specs=[pl.BlockSpec((tm, f), lambda j, i, te, nu: (i, 0)),
                          pl.BlockSpec((None, None, f, tn), lambda j, i, te, nu: (sl, te[i], 0, j))],
                out_specs=pl.BlockSpec((tm, tn), lambda j, i, te, nu: (i, j)),
                scratch_shapes=[pltpu.VMEM((f, tn), BF16)]),
            out_shape=jax.ShapeDtypeStruct((r, d), F32),
            compiler_params=_params(("arbitrary", "arbitrary"),
                                    2 * (tm * f * 2 + f * tn * 4 + tm * tn * 4) + f * tn * 2 + 2 * tm * tn * 4),
            name="swiglu_down",
        )(tile_expert, n_used, act, w_down)
    chunks = d // LANES
    tn = _tile(d, 1024, LANES)
    n_j = d // tn
    col = lambda i, j, nu: jnp.where(i < nu[0], j, n_j - 1)
    return pl.pallas_call(
        functools.partial(_down_tok_kernel, chunks=chunks),
        grid_spec=pltpu.PrefetchScalarGridSpec(
            num_scalar_prefetch=2, grid=(nt, n_j),
            in_specs=[pl.BlockSpec((tm, f), lambda i, j, te, nu: (used(i, nu), 0)),
                      pl.BlockSpec((None, None, f, tn), lambda i, j, te, nu: (sl, te[i], 0, col(i, j, nu)))],
            out_specs=pl.BlockSpec((tm * chunks, LANES), lambda i, j, te, nu: (i, 0))),
        out_shape=jax.ShapeDtypeStruct((r * chunks, LANES), F32),
        compiler_params=_params(("arbitrary", "arbitrary"),
                                2 * (tm * f * 2 + f * tn * 2 + tm * d * 4) + 4 * tm * tn * 4),
        name="swiglu_down_tok",
    )(tile_expert, n_used, act, w_down)


def _row_copy(src_hbm, src_row, dst, dst_row, sem, chunks):
    return pltpu.make_async_copy(
        src_hbm.at[pl.ds(pl.multiple_of(src_row * chunks, chunks), chunks), :],
        dst.at[pl.ds(pl.multiple_of(dst_row * chunks, chunks), chunks), :], sem)


def _start_rows(idx_ref, base, stride, n, src_hbm, dst, sem, chunks):
    def body(o, c):
        for u in range(DMA_UNROLL):
            r = o * DMA_UNROLL + u
            _row_copy(src_hbm, idx_ref[base + stride * r], dst, r, sem, chunks).start(priority=u % 2)
        return c
    lax.fori_loop(0, n // DMA_UNROLL, body, 0)


def _wait_rows(n, src_hbm, dst, sem, chunks):
    def body(o, c):
        for u in range(DMA_UNROLL):
            _row_copy(src_hbm, 0, dst, o * DMA_UNROLL + u, sem, chunks).wait()
        return c
    lax.fori_loop(0, n // DMA_UNROLL, body, 0)


def _gather_kernel(rt_ref, h_hbm, o_ref, buf, sems, *, chunks):
    i, n = pl.program_id(0), pl.num_programs(0)
    tg = o_ref.shape[0]
    slot = i % 2

    @pl.when(i == 0)
    def _():
        _start_rows(rt_ref, 0, 1, tg, h_hbm, buf.at[0], sems.at[0], chunks)

    @pl.when(i + 1 < n)
    def _():
        _start_rows(rt_ref, (i + 1) * tg, 1, tg, h_hbm, buf.at[1 - slot], sems.at[1 - slot], chunks)

    cur = buf.at[slot]
    _wait_rows(tg, h_hbm, cur, sems.at[slot], chunks)
    for s in range(chunks):
        o_ref[:, s * LANES:(s + 1) * LANES] = cur[pl.ds(s, tg, stride=chunks), :].astype(BF16)


def _gather_rows(h_tok, row_token, d):
    chunks = d // LANES
    r = row_token.shape[0]
    tg = _tile(r, 256, SUBLANES * DMA_UNROLL)
    return pl.pallas_call(
        functools.partial(_gather_kernel, chunks=chunks),
        grid_spec=pltpu.PrefetchScalarGridSpec(
            num_scalar_prefetch=1, grid=(r // tg,),
            in_specs=[pl.BlockSpec(memory_space=pl.ANY)],
            out_specs=pl.BlockSpec((tg, d), lambda i, rt: (i, 0)),
            scratch_shapes=[pltpu.VMEM((2, tg * chunks, LANES), F32), pltpu.SemaphoreType.DMA((2,))]),
        out_shape=jax.ShapeDtypeStruct((r, d), BF16),
        compiler_params=_params(("arbitrary",), 2 * tg * d * 4 + 2 * tg * d * 2 + 2 * tg * d * 4),
        name="moe_gather",
    )(row_token, h_tok)


def _combine_kernel(*refs, chunks, final):
    pos_ref, o_hbm, rt_ref, x_ref, g_ref, n_ref = refs[:6]
    rest = list(refs[6:])
    if not final:
        sc_ref, sh_ref = rest[:2]
        rest = rest[2:]
    xo_ref, y_ref, bufs, sems = rest
    i, n = pl.program_id(0), pl.num_programs(0)
    tc = x_ref.shape[0]
    slot = i % 2

    def start(step, sl):
        for k in range(TOP_K):
            _start_rows(pos_ref, TOP_K * step * tc + k, TOP_K, tc, o_hbm, bufs.at[sl, k], sems.at[sl, k], chunks)

    @pl.when(i == 0)
    def _():
        start(0, 0)

    @pl.when(i + 1 < n)
    def _():
        start(i + 1, 1 - slot)

    for k in range(TOP_K):
        _wait_rows(tc, o_hbm, bufs.at[slot, k], sems.at[slot, k], chunks)
    w0 = rt_ref[:, TOP_K:TOP_K + 1]
    w1 = rt_ref[:, TOP_K + 1:TOP_K + 2]
    b0, b1 = bufs.at[slot, 0], bufs.at[slot, 1]
    cols = [w0 * b0[pl.ds(s, tc, stride=chunks), :] + w1 * b1[pl.ds(s, tc, stride=chunks), :]
            for s in range(chunks)]
    x = x_ref[...] + g_ref[...] * jnp.concatenate(cols, axis=1)
    xo_ref[...] = x
    y = _rms(x) * n_ref[...]
    if final:
        y_ref[...] = y
    else:
        y_ref[...] = (y * (1.0 + sc_ref[...]) + sh_ref[...]).astype(BF16)


def _combine(o_tok, pos, route, x, g, n, sc, sh, grp, *, final):
    m, d = x.shape
    chunks = d // LANES
    tc = _tile(m, 256, SUBLANES * DMA_UNROLL)
    row = pl.BlockSpec((tc, d), lambda i, p: (i, 0))
    gspec = pl.BlockSpec((None, 1, d), lambda i, p: (grp(i), 0, 0))
    args = [pos, o_tok, route, x, g, n.reshape(1, d)]
    specs = [pl.BlockSpec(memory_space=pl.ANY), pl.BlockSpec((tc, LANES), lambda i, p: (i, 0)), row, gspec,
             pl.BlockSpec((1, d), lambda i, p: (0, 0))]
    if not final:
        args += [sc, sh]
        specs += [gspec, gspec]
    return pl.pallas_call(
        functools.partial(_combine_kernel, chunks=chunks, final=final),
        grid_spec=pltpu.PrefetchScalarGridSpec(
            num_scalar_prefetch=1, grid=(m // tc,), in_specs=specs,
            out_specs=[row, row],
            scratch_shapes=[pltpu.VMEM((2, TOP_K, tc * chunks, LANES), F32),
                            pltpu.SemaphoreType.DMA((2, TOP_K))]),
        out_shape=[jax.ShapeDtypeStruct((m, d), F32), jax.ShapeDtypeStruct((m, d), F32 if final else BF16)],
        compiler_params=_params(("arbitrary",), 2 * TOP_K * tc * d * 4 + 6 * tc * d * 4 + 6 * tc * d * 4),
        name="moe_combine",
    )(*args)


def _route_plan(route, n_exp, tm):
    m = route.shape[0]
    flat_e = route[:, :TOP_K].astype(jnp.int32).reshape(-1)
    onehot = (flat_e[:, None] == jnp.arange(n_exp)[None, :]).astype(jnp.int32)
    csum = jnp.cumsum(onehot, axis=0)
    rank = jnp.take_along_axis(csum, flat_e[:, None], axis=1)[:, 0] - 1
    counts = csum[-1]
    padded = (counts + tm - 1) // tm * tm
    starts = jnp.cumsum(padded) - padded
    pos = starts[flat_e] + rank
    n_rows = TOP_K * m + n_exp * tm
    row_token = jnp.zeros((n_rows,), jnp.int32).at[pos].set(jnp.arange(TOP_K * m, dtype=jnp.int32) // TOP_K)
    tiles = jnp.arange(n_rows // tm, dtype=jnp.int32)
    tile_expert = jnp.sum(tiles[:, None] * tm >= starts[None, :], axis=1).astype(jnp.int32) - 1
    n_used = (jnp.sum(padded) // tm).astype(jnp.int32).reshape(1)
    return pos.astype(jnp.int32), row_token, tile_expert, n_used


def _rope_tables(dec_seq, hd):
    half = hd // 4
    rows = dec_seq // GRID_W
    row_pos = jnp.broadcast_to(jnp.arange(rows)[:, None], (rows, GRID_W)).reshape(-1).astype(F32)
    col_pos = jnp.broadcast_to(jnp.arange(GRID_W)[None, :], (rows, GRID_W)).reshape(-1).astype(F32)
    inv_freq = ROPE_THETA ** (-jnp.arange(0, 2 * half, 2, dtype=F32) / (2 * half))
    ang_r = row_pos[:, None] * inv_freq[None, :]
    ang_c = col_pos[:, None] * inv_freq[None, :]
    cos = jnp.concatenate([jnp.cos(ang_r)] * 2 + [jnp.cos(ang_c)] * 2, axis=1)
    sin = jnp.concatenate([-jnp.sin(ang_r), jnp.sin(ang_r), -jnp.sin(ang_c), jnp.sin(ang_c)], axis=1)
    return cos, sin


def kernel(x_prompt, x_sample, c, cache_k, cache_v, c_ctx, w_mod, b_mod, norm1_g, norm2_g, w_in,
           q_norm_g, k_norm_g, w_pool, pool_scale, w_out, dense_w_gate, dense_w_up, dense_w_down,
           router_w, moe_w_gate, moe_w_up, moe_w_down, final_norm_g):
    batch, seq, d = x_prompt.shape
    dec_batch, dec_seq, _ = x_sample.shape
    depth = w_mod.shape[0]
    n_kv, hd = cache_k.shape[3], cache_k.shape[4]
    kvw = n_kv * hd
    qw = d
    pw = pool_scale.shape[1]
    n_ctx = batch * seq
    m = n_ctx + dec_batch * dec_seq
    dims = dict(n_ctx=n_ctx, seq=seq, dec_seq=dec_seq, qw=qw, kvw=kvw, pw=pw, hd=hd)
    n_exp = router_w.shape[2]
    gw = w_in.shape[2] - qw - 2 * kvw - pw
    assert hd == LANES and d % LANES == 0 and gw == 2 * d

    def grp_for(tile_rows):
        return lambda i: jnp.maximum((i * tile_rows - n_ctx) // dec_seq + 1, 0)

    n_grp = 1 + dec_batch
    rows = -(-n_grp // SUBLANES) * SUBLANES
    cs = jnp.zeros((rows, d), F32).at[0].set(c_ctx).at[1:n_grp].set(c)
    mod = _modulation(cs, w_mod, b_mod)[:, :n_grp]
    mod = mod.reshape(depth, n_grp, N_MOD, 1, d).transpose(0, 2, 1, 3, 4)

    tabs = _rope_tables(dec_seq, hd)
    bands = _pool_bands(min(POOL_BLOCK, seq))
    w_out_b = w_out.astype(BF16)
    moe_w_down_b = moe_w_down.astype(BF16)

    x = jnp.concatenate([x_prompt.reshape(n_ctx, d), x_sample.reshape(-1, d)], axis=0)
    tm_rn = _tile(m, 512, SUBLANES)
    tm_op = _tile(math.gcd(m, n_ctx), 256, SUBLANES)
    _, h = _resid_norm(x, None, None, norm1_g[0], mod[0, 1], mod[0, 0], grp_for(tm_rn), final=False)

    new_k, new_v = [], []
    for l in range(depth):
        sh1, sc1, g1, sh2, sc2, g2 = (mod[l, k] for k in range(N_MOD))
        q = _proj(h, w_in, l, 0, qw, dims, mode="q", gain=q_norm_g[l], tabs=tabs)
        k, kc = _proj(h, w_in, l, qw, kvw, dims, mode="k", gain=k_norm_g[l], tabs=tabs, with_cache=True)
        v, vc = _proj(h, w_in, l, qw + kvw, kvw, dims, mode="v", with_cache=True)
        u = _proj(h, w_in, l, qw + 2 * kvw, pw, dims, mode="u")
        gates = _proj(h, w_in, l, qw + 2 * kvw + pw, gw, dims, mode="gate")
        new_k.append(kc.reshape(batch, seq, n_kv, hd))
        new_v.append(vc.reshape(batch, seq, n_kv, hd))
        attn_ctx, attn_lat = _attention(q, k, v, cache_k, cache_v, l, dims)
        pool = _pool(u, bands, w_pool, pool_scale[l:l + 1], l, dims)
        last = l == depth - 1
        if last:
            n_next, sc_next, sh_next = final_norm_g, None, None
        else:
            n_next, sc_next, sh_next = norm1_g[l + 1], mod[l + 1, 1], mod[l + 1, 0]
        j = l // 2
        if l % 2 == 0:
            x, h2 = _outproj(attn_ctx, attn_lat, pool, gates, x, w_out_b[l], g1, norm2_g[l], sc2, sh2, grp_for(tm_op), None)
            tm = _tile(m, 1024, SUBLANES)
            nt = jnp.full((1,), m // tm, jnp.int32)
            o = _swiglu(h2, jnp.zeros((m // tm,), jnp.int32), nt, tm,
                        dense_w_gate[:, None], dense_w_up[:, None], dense_w_down[:, None], j, token_major=False)
            x, h = _resid_norm(x, o, g2, n_next, sc_next, sh_next, grp_for(tm_rn), final=last)
        else:
            x, h2_tok, route = _outproj(attn_ctx, attn_lat, pool, gates, x, w_out_b[l], g1, norm2_g[l], sc2, sh2,
                                        grp_for(tm_op), router_w[j])
            tm = _tile(m, 512, SUBLANES)
            pos, row_token, tile_expert, n_used = _route_plan(route, n_exp, tm)
            xs = _gather_rows(h2_tok, row_token, d)
            o_tok = _swiglu(xs, tile_expert, n_used, tm, moe_w_gate, moe_w_up, moe_w_down_b, j, token_major=True)
            x, h = _combine(o_tok, pos, route, x, g2, n_next, sc_next, sh_next, grp_for(tm_op), final=last)

    y_prompt = h[:n_ctx].reshape(batch, seq, d)
    y_sample = h[n_ctx:].reshape(dec_batch, dec_seq, d)
    return (y_prompt, y_sample, jnp.stack(new_k, axis=1), jnp.stack(new_v, axis=1))
```

```python
import functools
import math

import jax
import jax.numpy as jnp
from jax import lax
from jax.experimental import pallas as pl
from jax.experimental.pallas import tpu as pltpu

GRID_W = 64
ROPE_THETA = 10000.0
POOL_WINDOWS = (2, 4, 8, 16)
TOP_K = 2
N_MOD = 6
EPS = 1e-6

LANES = 128
SUBLANES = 8
VMEM_BUDGET = 60000 * 1024
POOL_BLOCK = 256
ROW_CHUNK = 256
DMA_UNROLL = 8
QK_SCALE_LOG2E = math.log2(math.e)
ATTN_ROWS = 512
OUTPROJ_CHUNK = 128
ATTN_LOOKAHEAD = 2

F32 = jnp.float32
BF16 = jnp.bfloat16


def _params(semantics, vmem_bytes):
    limit = int(min(VMEM_BUDGET, vmem_bytes * 5 // 4 + (4 << 20)))
    return pltpu.CompilerParams(dimension_semantics=semantics, vmem_limit_bytes=limit)


def _tile(n, pref, align):
    t = min(n, pref)
    t -= t % align
    while t > align and n % t:
        t -= align
    assert t >= align and n % t == 0, (n, pref, align)
    return t


def _rms(x):
    return x * lax.rsqrt(jnp.mean(x * x, axis=-1, keepdims=True) + EPS)


def _sigmoid(x):
    return 1.0 / (1.0 + jnp.exp(-x))


def _mod_kernel(c_ref, w_ref, b_ref, o_ref):
    c = c_ref[...]
    s = c * _sigmoid(c)
    o_ref[...] = jnp.dot(s, w_ref[...], preferred_element_type=F32,
                         precision=lax.Precision.HIGHEST) + b_ref[...]


def _modulation(cs, w_mod, b_mod):
    depth, d, n = w_mod.shape
    rows = cs.shape[0]
    tn = _tile(n, 1024, LANES)
    return pl.pallas_call(
        _mod_kernel,
        grid=(depth, n // tn),
        in_specs=[pl.BlockSpec((rows, d), lambda l, j: (0, 0)),
                  pl.BlockSpec((None, d, tn), lambda l, j: (l, 0, j)),
                  pl.BlockSpec((None, 1, tn), lambda l, j: (l, 0, j))],
        out_specs=pl.BlockSpec((None, rows, tn), lambda l, j: (l, 0, j)),
        out_shape=jax.ShapeDtypeStruct((depth, rows, n), F32),
        compiler_params=_params(("arbitrary", "arbitrary"), 2 * d * tn * 4 + 4 * rows * (d + tn) * 4),
        name="modulation",
    )(cs, w_mod, b_mod.reshape(depth, 1, n))


def _resid_norm_kernel(*refs, has_res, final):
    it = iter(refs)
    x_ref = next(it)
    if has_res:
        o_ref, g_ref = next(it), next(it)
    n_ref = next(it)
    if not final:
        sc_ref, sh_ref = next(it), next(it)
    outs = list(it)
    x = x_ref[...]
    if has_res:
        x = x + g_ref[...] * o_ref[...]
        outs[0][...] = x
        outs = outs[1:]
    y = _rms(x) * n_ref[...]
    if final:
        outs[0][...] = y
    else:
        outs[0][...] = (y * (1.0 + sc_ref[...]) + sh_ref[...]).astype(BF16)


def _group_spec(d, grp):
    return pl.BlockSpec((None, 1, d), lambda i: (grp(i), 0, 0))


def _resid_norm(x, o, g, n, sc, sh, grp, *, final):
    m, d = x.shape
    tm = _tile(m, 512, SUBLANES)
    has_res = o is not None
    row = pl.BlockSpec((tm, d), lambda i: (i, 0))
    vec = pl.BlockSpec((1, d), lambda i: (0, 0))
    args, specs = [x], [row]
    if has_res:
        args += [o, g]
        specs += [row, _group_spec(d, grp)]
    args.append(n.reshape(1, d))
    specs.append(vec)
    if not final:
        args += [sc, sh]
        specs += [_group_spec(d, grp)] * 2
    out_shape, out_specs = [], []
    if has_res:
        out_shape.append(jax.ShapeDtypeStruct((m, d), F32))
        out_specs.append(row)
    out_shape.append(jax.ShapeDtypeStruct((m, d), F32 if final else BF16))
    out_specs.append(row)
    res = pl.pallas_call(
        functools.partial(_resid_norm_kernel, has_res=has_res, final=final),
        grid=(m // tm,), in_specs=specs, out_specs=out_specs, out_shape=out_shape,
        compiler_params=_params(("arbitrary",), 8 * tm * d * 4),
        name="resid_norm",
    )(*args)
    return res if has_res else (None, res[0])


def _embed_norm_kernel(xp_ref, xs_ref, n_ref, sc_ref, sh_ref, x_ref, h_ref, *, nct):
    x = jnp.where(pl.program_id(0) < nct, xp_ref[...], xs_ref[...])
    x_ref[...] = x
    h_ref[...] = (_rms(x) * n_ref[...] * (1.0 + sc_ref[...]) + sh_ref[...]).astype(BF16)


def _embed_norm(xp, xs, n, sc, sh, grp):
    n_ctx, d = xp.shape
    m = n_ctx + xs.shape[0]
    tm = _tile(math.gcd(m, n_ctx), 512, SUBLANES)
    nct = n_ctx // tm
    row = pl.BlockSpec((tm, d), lambda i: (i, 0))
    return pl.pallas_call(
        functools.partial(_embed_norm_kernel, nct=nct),
        grid=(m // tm,),
        in_specs=[pl.BlockSpec((tm, d), lambda i: (jnp.minimum(i, nct - 1), 0)),
                  pl.BlockSpec((tm, d), lambda i: (jnp.maximum(i - nct, 0), 0)),
                  pl.BlockSpec((1, d), lambda i: (0, 0)), _group_spec(d, grp), _group_spec(d, grp)],
        out_specs=[row, row],
        out_shape=[jax.ShapeDtypeStruct((m, d), F32), jax.ShapeDtypeStruct((m, d), BF16)],
        compiler_params=_params(("arbitrary",), 10 * tm * d * 4),
        name="embed_norm",
    )(xp, xs, n.reshape(1, d), sc, sh)


def _rope(y, cos, sin):
    half = LANES // 4
    lane = lax.broadcasted_iota(jnp.int32, y.shape, 1)
    partner = jnp.where((lane & (2 * half - 1)) < half,
                        pltpu.roll(y, LANES - half, axis=1), pltpu.roll(y, half, axis=1))
    return y * cos + partner * sin


def _cast_weights_once(w_ref, wb):
    @pl.when(pl.program_id(1) == 0)
    def _():
        wb[...] = w_ref[...].astype(BF16)


def _proj_chunks(h_ref, wb, epilogue):
    w = wb[...]
    rows = h_ref.shape[0]
    rsub = _tile(rows, ROW_CHUNK, SUBLANES)
    for rc in range(rows // rsub):
        sl = slice(rc * rsub, (rc + 1) * rsub)
        epilogue(jnp.dot(h_ref[sl, :], w, preferred_element_type=F32), sl)


def _qk_proj_kernel(h_ref, w_ref, g_ref, cos_ref, sin_ref, o_ref, *rest, nct, hd, mult):
    cache, wb = rest[:-1], rest[-1]
    _cast_weights_once(w_ref, wb)

    def run(is_ctx):
        def epilogue(acc, sl):
            cols = []
            for hh in range(acc.shape[1] // hd):
                y = _rms(acc[:, hh * hd:(hh + 1) * hd]) * g_ref[...]
                if not is_ctx:
                    y = _rope(y, cos_ref[sl, :], sin_ref[sl, :])
                cols.append(y)
            y = jnp.concatenate(cols, axis=1) if len(cols) > 1 else cols[0]
            if is_ctx and cache:
                cache[0][sl, :] = y
            o_ref[sl, :] = (y * mult if mult != 1.0 else y).astype(BF16)
        _proj_chunks(h_ref, wb, epilogue)

    i = pl.program_id(1)
    pl.when(i < nct)(lambda: run(True))
    pl.when(i >= nct)(lambda: run(False))


def _plain_proj_kernel(h_ref, w_ref, o_ref, *rest, nct, gate):
    cache, wb = rest[:-1], rest[-1]
    _cast_weights_once(w_ref, wb)

    def run(to_cache):
        def epilogue(acc, sl):
            o_ref[sl, :] = (_sigmoid(acc) if gate else acc).astype(BF16)
            if to_cache:
                cache[0][sl, :] = acc
        _proj_chunks(h_ref, wb, epilogue)

    if cache:
        i = pl.program_id(1)
        pl.when(i < nct)(lambda: run(True))
        pl.when(i >= nct)(lambda: run(False))
    else:
        run(False)


def _proj(h, w_in, layer, col0, width, dims, *, mode, gain=None, tabs=None, with_cache=False):
    m, d = h.shape
    n_ctx, dec_seq, hd = dims["n_ctx"], dims["dec_seq"], dims["hd"]
    tn = _tile(math.gcd(width, col0), 512, LANES)
    tm = _tile(dec_seq, 1024, SUBLANES)
    assert n_ctx % tm == 0 and col0 % tn == 0 and width % tn == 0
    nct, tps = n_ctx // tm, dec_seq // tm
    args = [h, w_in]
    specs = [pl.BlockSpec((tm, d), lambda j, i: (i, 0)),
             pl.BlockSpec((None, d, tn), lambda j, i: (layer, 0, col0 // tn + j))]
    if mode in ("q", "k"):
        tab = pl.BlockSpec((tm, hd), lambda j, i: (jnp.maximum(i - nct, 0) % tps, 0))
        args += [gain.reshape(1, hd), tabs[0], tabs[1]]
        specs += [pl.BlockSpec((1, hd), lambda j, i: (0, 0)), tab, tab]
        body = functools.partial(_qk_proj_kernel, nct=nct, hd=hd,
                                 mult=QK_SCALE_LOG2E * float(hd) ** -0.5 if mode == "q" else 1.0)
    else:
        body = functools.partial(_plain_proj_kernel, nct=nct, gate=mode == "gate")
    out_shape = [jax.ShapeDtypeStruct((m, width), BF16)]
    out_specs = [pl.BlockSpec((tm, tn), lambda j, i: (i, j))]
    if with_cache:
        out_shape.append(jax.ShapeDtypeStruct((n_ctx, width), F32))
        out_specs.append(pl.BlockSpec((tm, tn), lambda j, i: (jnp.minimum(i, nct - 1), j)))
    vmem = (2 * (tm * d * 2 + d * tn * 4 + 2 * tm * hd * 4 + tm * tn * 2 + tm * tn * 4)
            + d * tn * 2 + 8 * ROW_CHUNK * tn * 4)
    res = pl.pallas_call(
        body, grid=(width // tn, m // tm), in_specs=specs, out_specs=out_specs, out_shape=out_shape,
        scratch_shapes=[pltpu.VMEM((d, tn), BF16)],
        compiler_params=_params(("arbitrary", "arbitrary"), vmem), name="proj_" + mode,
    )(*args)
    return res if with_cache else res[0]


def _qk(q, k):
    return lax.dot_general(q, k, (((1,), (1,)), ((), ())), preferred_element_type=F32)


def _attn_ctx_kernel(q_ref, k_ref, v_ref, o_ref, *, n_kv, n_g, hd):
    for kv in range(n_kv):
        k = k_ref[:, kv * hd:(kv + 1) * hd]
        v = v_ref[:, kv * hd:(kv + 1) * hd]
        for g in range(n_g):
            c0 = (kv * n_g + g) * hd
            s = _qk(q_ref[:, c0:c0 + hd], k)
            p = jnp.exp2(s - jnp.max(s, axis=-1, keepdims=True))
            o = jnp.dot(p.astype(BF16), v, preferred_element_type=F32)
            o_ref[:, c0:c0 + hd] = (o / jnp.sum(p, axis=-1, keepdims=True)).astype(BF16)


def _attn_lat_kernel(q_ref, k_ref, v_ref, ck_ref, cv_ref, o_ref, *, n_g, hd):
    k, v = k_ref[...], v_ref[...]
    ck, cv = ck_ref[...].astype(BF16), cv_ref[...].astype(BF16)
    tdot = lambda a, b: lax.dot_general(a, b, (((0,), (0,)), ((), ())), preferred_element_type=F32)

    def scores(g):
        q = q_ref[:, g * hd:(g + 1) * hd]
        return _qk(k, q), _qk(ck, q)

    ahead = [scores(g) for g in range(min(ATTN_LOOKAHEAD, n_g))]
    for g in range(n_g):
        s1, s2 = ahead.pop(0)
        mx = jnp.maximum(jnp.max(s1, axis=0, keepdims=True), jnp.max(s2, axis=0, keepdims=True))
        p1, p2 = jnp.exp2(s1 - mx), jnp.exp2(s2 - mx)
        den = jnp.sum(p1, axis=0, keepdims=True) + jnp.sum(p2, axis=0, keepdims=True)
        if g + ATTN_LOOKAHEAD < n_g:
            ahead.append(scores(g + ATTN_LOOKAHEAD))
        o = (tdot(v, p1.astype(BF16)) + tdot(cv, p2.astype(BF16))) / den
        o_ref[:, g * hd:(g + 1) * hd] = o.T.astype(BF16)


def _attention(q, k, v, cache_k, cache_v, layer, dims):
    m, qw = q.shape
    n_ctx, seq, dec_seq, kvw, hd = (dims[key] for key in ("n_ctx", "seq", "dec_seq", "kvw", "hd"))
    n_kv = kvw // hd
    n_g = qw // kvw
    dec_batch, _, past = cache_k.shape[:3]
    ck = cache_k.reshape(dec_batch, cache_k.shape[1], past, kvw)
    cv = cache_v.reshape(dec_batch, cache_v.shape[1], past, kvw)

    ctx = pl.pallas_call(
        functools.partial(_attn_ctx_kernel, n_kv=n_kv, n_g=n_g, hd=hd),
        grid=(n_ctx // seq,),
        in_specs=[pl.BlockSpec((seq, qw), lambda b: (b, 0)),
                  pl.BlockSpec((seq, kvw), lambda b: (b, 0)),
                  pl.BlockSpec((seq, kvw), lambda b: (b, 0))],
        out_specs=pl.BlockSpec((seq, qw), lambda b: (b, 0)),
        out_shape=jax.ShapeDtypeStruct((n_ctx, qw), BF16),
        compiler_params=_params(("arbitrary",), 4 * seq * (qw + kvw) * 2 + 8 * seq * seq * 4),
        name="attn_ctx",
    )(q, k, v)

    assert n_ctx % dec_seq == 0
    tq = _tile(dec_seq, ATTN_ROWS, SUBLANES)
    gw = n_g * hd
    qrow = lambda b, kv, t: ((n_ctx + b * dec_seq) // tq + t, kv)
    krow = lambda b, kv, t: (n_ctx // dec_seq + b, kv)
    crow = lambda b, kv, t: (b, layer, 0, kv)
    lat = pl.pallas_call(
        functools.partial(_attn_lat_kernel, n_g=n_g, hd=hd),
        grid=(dec_batch, n_kv, dec_seq // tq),
        in_specs=[pl.BlockSpec((tq, gw), qrow),
                  pl.BlockSpec((dec_seq, hd), krow), pl.BlockSpec((dec_seq, hd), krow),
                  pl.BlockSpec((None, None, past, hd), crow), pl.BlockSpec((None, None, past, hd), crow)],
        out_specs=pl.BlockSpec((tq, gw), lambda b, kv, t: (b * (dec_seq // tq) + t, kv)),
        out_shape=jax.ShapeDtypeStruct((m - n_ctx, qw), BF16),
        compiler_params=_params(("arbitrary",) * 3,
                                4 * tq * gw * 2 + 4 * dec_seq * hd * 2 + 4 * past * hd * 4
                                + 6 * tq * (dec_seq + past) * 4),
        name="attn_lat",
    )(q, k, v, ck, cv)
    return ctx, lat


def _pool_kernel(u_ref, band_ref, w_ref, sc_ref, o_ref, *, nct, seq, dec_seq, windows):
    i, g = pl.program_id(0), pl.program_id(1)
    rows = u_ref.shape[0]
    pb = band_ref.shape[-1]
    w = w_ref[...].astype(BF16)
    half = jnp.zeros((), jnp.int32)
    for gi, win in enumerate(windows):
        half = jnp.where(g == gi, win // 2, half)

    def run(length):
        for b in range(rows // pb):
            p0 = (b * pb) % length
            cur = u_ref[b * pb:(b + 1) * pb, :]
            tot = jnp.dot(band_ref[1], cur, preferred_element_type=F32)
            if p0 > 0:
                tot += jnp.dot(band_ref[0], u_ref[(b - 1) * pb:b * pb, :], preferred_element_type=F32)
            if p0 + pb < length:
                tot += jnp.dot(band_ref[2], u_ref[(b + 1) * pb:(b + 2) * pb, :], preferred_element_type=F32)
            pos = p0 + lax.broadcasted_iota(jnp.int32, (pb, 1), 0)
            cnt = jnp.minimum(pos + half, length) - jnp.maximum(pos - half, 0)
            pooled = tot / cnt.astype(F32) - cur.astype(F32)
            mixed = jnp.dot(pooled.astype(BF16), w, preferred_element_type=F32)
            o_ref[b * pb:(b + 1) * pb, :] = (mixed * sc_ref[...]).astype(BF16)

    @pl.when(i < nct)
    def _():
        run(seq)

    @pl.when(i >= nct)
    def _():
        run(dec_seq)


def _pool_bands(pb):
    t = jnp.arange(pb)[:, None]
    j = jnp.arange(pb)[None, :]
    out = []
    for win in POOL_WINDOWS:
        h = win // 2
        prev = (j - pb - t) >= -h
        cur = ((j - t) >= -h) & ((j - t) <= h - 1)
        nxt = (j + pb - t) <= h - 1
        out.append(jnp.stack([prev, cur, nxt]))
    return jnp.stack(out).astype(BF16)


def _pool(u, bands, w_pool, pool_scale, layer, dims):
    m, pw = u.shape
    n_ctx, seq, dec_seq = dims["n_ctx"], dims["seq"], dims["dec_seq"]
    ng = len(POOL_WINDOWS)
    gd = pw // ng
    pb = bands.shape[-1]
    tr = dec_seq
    assert n_ctx % tr == 0 and tr % seq == 0 and seq % pb == 0 and max(POOL_WINDOWS) // 2 <= pb
    return pl.pallas_call(
        functools.partial(_pool_kernel, nct=n_ctx // tr, seq=seq, dec_seq=dec_seq, windows=POOL_WINDOWS),
        grid=(m // tr, ng),
        in_specs=[pl.BlockSpec((tr, gd), lambda i, g: (i, g)),
                  pl.BlockSpec((None, 3, pb, pb), lambda i, g: (g, 0, 0, 0)),
                  pl.BlockSpec((None, None, gd, gd), lambda i, g: (layer, g, 0, 0)),
                  pl.BlockSpec((1, gd), lambda i, g: (0, g))],
        out_specs=pl.BlockSpec((tr, gd), lambda i, g: (i, g)),
        out_shape=jax.ShapeDtypeStruct((m, pw), BF16),
        compiler_params=_params(("arbitrary", "arbitrary"),
                                4 * tr * gd * 2 + 2 * gd * gd * 4 + 6 * pb * pb * 2 + 8 * pb * gd * 4),
        name="pool_mix",
    )(u, bands, w_pool, pool_scale.reshape(1, pw))


def _outproj_kernel(*refs, n_exp, nct):
    (ac_ref, al_ref, p_ref, ga_ref, gb_ref, x_ref, w_ref, g1_ref, n2_ref, sc_ref, sh_ref) = refs[:11]
    rest = refs[11:]
    rows, d = x_ref.shape
    rsub = _tile(rows, OUTPROJ_CHUNK, SUBLANES) if n_exp else rows
    is_ctx = pl.program_id(0) < nct
    w = w_ref[...]
    xs, h2s = [], []
    for r0 in range(0, rows, rsub):
        sl = slice(r0, r0 + rsub)
        attn = jnp.where(is_ctx, ac_ref[sl, :], al_ref[sl, :])
        merged = (ga_ref[sl, :].astype(F32) * attn.astype(F32)
                  + gb_ref[sl, :].astype(F32) * p_ref[sl, :].astype(F32)).astype(BF16)
        x = x_ref[sl, :] + g1_ref[...] * jnp.dot(merged, w, preferred_element_type=F32)
        xs.append(x)
        h2s.append(_rms(x) * n2_ref[...] * (1.0 + sc_ref[...]) + sh_ref[...])
    if n_exp == 0:
        xo_ref, h_ref = rest
        for c, r0 in enumerate(range(0, rows, rsub)):
            xo_ref[r0:r0 + rsub, :] = xs[c]
            h_ref[r0:r0 + rsub, :] = h2s[c].astype(BF16)
        return
    rw_ref, xo_ref, h_ref, rt_ref = rest
    for c, r0 in enumerate(range(0, rows, rsub)):
        xo_ref[r0:r0 + rsub, :] = xs[c]
    h2 = jnp.concatenate(h2s, axis=0) if len(h2s) > 1 else h2s[0]
    chunks = d // LANES
    for s in range(chunks):
        h_ref[pl.ds(s, rows, stride=chunks), :] = h2[:, s * LANES:(s + 1) * LANES]
    h_hi = h2.astype(BF16)
    h_lo = (h2 - h_hi.astype(F32)).astype(BF16)
    part = (jnp.dot(h_hi, rw_ref[...], preferred_element_type=F32)
            + jnp.dot(h_lo, rw_ref[...], preferred_element_type=F32))
    logits = part[:, :LANES] + part[:, LANES:]
    lane = lax.broadcasted_iota(jnp.int32, logits.shape, 1)
    lane_f = lane.astype(F32)
    neg = jnp.float32(-jnp.inf)
    lg = jnp.where(lane < n_exp, logits, neg)
    m1 = jnp.max(lg, axis=-1, keepdims=True)
    i1 = jnp.min(jnp.where(lg == m1, lane_f, float(LANES)), axis=-1, keepdims=True)
    lg2 = jnp.where(lane_f == i1, neg, lg)
    m2 = jnp.max(lg2, axis=-1, keepdims=True)
    i2 = jnp.min(jnp.where(lg2 == m2, lane_f, float(LANES)), axis=-1, keepdims=True)
    e2 = jnp.exp(m2 - m1)
    den = 1.0 + e2
    rt = jnp.where(lane == 0, i1, 0.0)
    rt = jnp.where(lane == 1, i2, rt)
    rt = jnp.where(lane == 2, 1.0 / den, rt)
    rt_ref[...] = jnp.where(lane == 3, e2 / den, rt)


def _outproj(attn_ctx, attn_lat, pool, gates, x, w_out_b, g1, n2, sc2, sh2, grp, router_w):
    m, d = x.shape
    tm = _tile(math.gcd(m, attn_ctx.shape[0]), 256, SUBLANES)
    nct = attn_ctx.shape[0] // tm
    chunks = d // LANES
    row = pl.BlockSpec((tm, d), lambda i: (i, 0))
    vec = pl.BlockSpec((1, d), lambda i: (0, 0))
    args = [attn_ctx, attn_lat, pool, gates, gates, x, w_out_b, g1, n2.reshape(1, d), sc2, sh2]
    specs = [pl.BlockSpec((tm, d), lambda i: (jnp.minimum(i, nct - 1), 0)),
             pl.BlockSpec((tm, d), lambda i: (jnp.maximum(i - nct, 0), 0)),
             row, row, pl.BlockSpec((tm, d), lambda i: (i, 1)), row,
             pl.BlockSpec((d, d), lambda i: (0, 0)), _group_spec(d, grp), vec,
             _group_spec(d, grp), _group_spec(d, grp)]
    out_shape = [jax.ShapeDtypeStruct((m, d), F32)]
    out_specs = [row]
    n_exp = 0
    if router_w is None:
        out_shape.append(jax.ShapeDtypeStruct((m, d), BF16))
        out_specs.append(row)
    else:
        n_exp = router_w.shape[1]
        assert TOP_K == 2 and 2 * TOP_K <= LANES and n_exp <= LANES
        rw = jnp.pad(router_w, ((0, 0), (0, LANES - n_exp)))
        rw_hi = rw.astype(BF16)
        rw_lo = (rw - rw_hi.astype(F32)).astype(BF16)
        args.append(jnp.concatenate([rw_hi, rw_lo], axis=1))
        specs.append(pl.BlockSpec((d, 2 * LANES), lambda i: (0, 0)))
        out_shape += [jax.ShapeDtypeStruct((m * chunks, LANES), F32), jax.ShapeDtypeStruct((m, LANES), F32)]
        out_specs += [pl.BlockSpec((tm * chunks, LANES), lambda i: (i, 0)),
                      pl.BlockSpec((tm, LANES), lambda i: (i, 0))]
    return pl.pallas_call(
        functools.partial(_outproj_kernel, n_exp=n_exp, nct=nct),
        grid=(m // tm,), in_specs=specs, out_specs=out_specs, out_shape=out_shape,
        compiler_params=_params(("arbitrary",), 2 * tm * d * (4 * 2 + 3 * 4) + 2 * d * d * 2 + 8 * tm * d * 4),
        name="outproj_route" if n_exp else "outproj",
    )(*args)


def _changed(te_ref, i):
    return (i == 0) | (te_ref[i] != te_ref[jnp.maximum(i - 1, 0)])


def _up_kernel(te_ref, nu_ref, x_ref, wg_ref, wu_ref, o_ref, wgb, wub):
    i = pl.program_id(1)

    @pl.when(_changed(te_ref, i))
    def _():
        wgb[...] = wg_ref[...].astype(BF16)
        wub[...] = wu_ref[...].astype(BF16)

    @pl.when(i < nu_ref[0])
    def _():
        x = x_ref[...]
        a = jnp.dot(x, wgb[...], preferred_element_type=F32)
        b = jnp.dot(x, wub[...], preferred_element_type=F32)
        o_ref[...] = (a * _sigmoid(a) * b).astype(BF16)

    @pl.when(i >= nu_ref[0])
    def _():
        o_ref[...] = jnp.zeros(o_ref.shape, BF16)


def _down_kernel(te_ref, nu_ref, a_ref, w_ref, o_ref, wb):
    i = pl.program_id(1)

    @pl.when(_changed(te_ref, i))
    def _():
        wb[...] = w_ref[...].astype(BF16)

    o_ref[...] = jnp.dot(a_ref[...], wb[...], preferred_element_type=F32)


def _down_tok_kernel(te_ref, nu_ref, a_ref, w_ref, o_ref, *, chunks):
    i, j = pl.program_id(0), pl.program_id(1)
    tm = a_ref.shape[0]
    per = w_ref.shape[1] // LANES

    @pl.when(i < nu_ref[0])
    def _():
        acc = jnp.dot(a_ref[...], w_ref[...], preferred_element_type=F32)
        for jj in range(chunks // per):
            @pl.when(j == jj)
            def _():
                for c in range(per):
                    o_ref[pl.ds(jj * per + c, tm, stride=chunks), :] = acc[:, c * LANES:(c + 1) * LANES]

    @pl.when((i >= nu_ref[0]) & (j == 0))
    def _():
        o_ref[...] = jnp.zeros(o_ref.shape, F32)


def _swiglu(xs, tile_expert, n_used, tm, w_gate, w_up, w_down, sl, *, token_major):
    r, d = xs.shape
    f = w_gate.shape[3]
    tf = _tile(f, 512, LANES)
    tn = _tile(d, 512, LANES)
    nt = r // tm
    used = lambda i, nu: jnp.minimum(i, nu[0] - 1)
    act = pl.pallas_call(
        _up_kernel,
        grid_spec=pltpu.PrefetchScalarGridSpec(
            num_scalar_prefetch=2, grid=(f // tf, nt),
            in_specs=[pl.BlockSpec((tm, d), lambda j, i, te, nu: (used(i, nu), 0)),
                      pl.BlockSpec((None, None, d, tf), lambda j, i, te, nu: (sl, te[i], 0, j)),
                      pl.BlockSpec((None, None, d, tf), lambda j, i, te, nu: (sl, te[i], 0, j))],
            out_specs=pl.BlockSpec((tm, tf), lambda j, i, te, nu: (i, j)),
            scratch_shapes=[pltpu.VMEM((d, tf), BF16), pltpu.VMEM((d, tf), BF16)]),
        out_shape=jax.ShapeDtypeStruct((r, f), BF16),
        compiler_params=_params(("arbitrary", "arbitrary"),
                                2 * (tm * d * 2 + 2 * d * tf * 4 + tm * tf * 2) + 2 * d * tf * 2 + 4 * tm * tf * 4),
        name="swiglu_up",
    )(tile_expert, n_used, xs, w_gate, w_up)
    if not token_major:
        return pl.pallas_call(
            _down_kernel,
            grid_spec=pltpu.PrefetchScalarGridSpec(
                num_scalar_prefetch=2, grid=(d // tn, nt),
                in_specs=[pl.BlockSpec((tm, f), lambda j, i, te, nu: (i, 0)),
                          pl.BlockSpec((None, None, f, tn), lambda j, i, te, nu: (sl, te[i], 0, j))],
                out_specs=pl.BlockSpec((tm, tn), lambda j, i, te, nu: (i, j)),
                scratch_shapes=[pltpu.VMEM((f, tn), BF16)]),
            out_shape=jax.ShapeDtypeStruct((r, d), F32),
            compiler_params=_params(("arbitrary", "arbitrary"),
                                    2 * (tm * f * 2 + f * tn * 4 + tm * tn * 4) + f * tn * 2 + 2 * tm * tn * 4),
            name="swiglu_down",
        )(tile_expert, n_used, act, w_down)
    chunks = d // LANES
    tn = _tile(d, 1024, LANES)
    n_j = d // tn
    col = lambda i, j, nu: jnp.where(i < nu[0], j, n_j - 1)
    return pl.pallas_call(
        functools.partial(_down_tok_kernel, chunks=chunks),
        grid_spec=pltpu.PrefetchScalarGridSpec(
            num_scalar_prefetch=2, grid=(nt, n_j),
            in_specs=[pl.BlockSpec((tm, f), lambda i, j, te, nu: (used(i, nu), 0)),
                      pl.BlockSpec((None, None, f, tn), lambda i, j, te, nu: (sl, te[i], 0, col(i, j, nu)))],
            out_specs=pl.BlockSpec((tm * chunks, LANES), lambda i, j, te, nu: (i, 0))),
        out_shape=jax.ShapeDtypeStruct((r * chunks, LANES), F32),
        compiler_params=_params(("arbitrary", "arbitrary"),
                                2 * (tm * f * 2 + f * tn * 2 + tm * d * 4) + 4 * tm * tn * 4),
        name="swiglu_down_tok",
    )(tile_expert, n_used, act, w_down)


def _row_copy(src_hbm, src_row, dst, dst_row, sem, chunks):
    return pltpu.make_async_copy(
        src_hbm.at[pl.ds(pl.multiple_of(src_row * chunks, chunks), chunks), :],
        dst.at[pl.ds(pl.multiple_of(dst_row * chunks, chunks), chunks), :], sem)


def _start_rows(idx_ref, base, stride, n, src_hbm, dst, sem, chunks):
    def body(o, c):
        for u in range(DMA_UNROLL):
            r = o * DMA_UNROLL + u
            _row_copy(src_hbm, idx_ref[base + stride * r], dst, r, sem, chunks).start(priority=u % 2)
        return c
    lax.fori_loop(0, n // DMA_UNROLL, body, 0)


def _wait_rows(n, src_hbm, dst, sem, chunks):
    def body(o, c):
        for u in range(DMA_UNROLL):
            _row_copy(src_hbm, 0, dst, o * DMA_UNROLL + u, sem, chunks).wait()
        return c
    lax.fori_loop(0, n // DMA_UNROLL, body, 0)


def _gather_kernel(rt_ref, h_hbm, o_ref, buf, sems, *, chunks):
    i, n = pl.program_id(0), pl.num_programs(0)
    tg = o_ref.shape[0]
    slot = i % 2

    @pl.when(i == 0)
    def _():
        _start_rows(rt_ref, 0, 1, tg, h_hbm, buf.at[0], sems.at[0], chunks)

    @pl.when(i + 1 < n)
    def _():
        _start_rows(rt_ref, (i + 1) * tg, 1, tg, h_hbm, buf.at[1 - slot], sems.at[1 - slot], chunks)

    cur = buf.at[slot]
    _wait_rows(tg, h_hbm, cur, sems.at[slot], chunks)
    for s in range(chunks):
        o_ref[:, s * LANES:(s + 1) * LANES] = cur[pl.ds(s, tg, stride=chunks), :].astype(BF16)


def _gather_rows(h_tok, row_token, d):
    chunks = d // LANES
    r = row_token.shape[0]
    tg = _tile(r, 256, SUBLANES * DMA_UNROLL)
    return pl.pallas_call(
        functools.partial(_gather_kernel, chunks=chunks),
        grid_spec=pltpu.PrefetchScalarGridSpec(
            num_scalar_prefetch=1, grid=(r // tg,),
            in_specs=[pl.BlockSpec(memory_space=pl.ANY)],
            out_specs=pl.BlockSpec((tg, d), lambda i, rt: (i, 0)),
            scratch_shapes=[pltpu.VMEM((2, tg * chunks, LANES), F32), pltpu.SemaphoreType.DMA((2,))]),
        out_shape=jax.ShapeDtypeStruct((r, d), BF16),
        compiler_params=_params(("arbitrary",), 2 * tg * d * 4 + 2 * tg * d * 2 + 2 * tg * d * 4),
        name="moe_gather",
    )(row_token, h_tok)


def _combine_kernel(*refs, chunks, final, nct):
    pos_ref, o_hbm, rt_ref, x_ref, g_ref, n_ref = refs[:6]
    rest = list(refs[6:])
    if not final:
        sc_ref, sh_ref = rest[:2]
        rest = rest[2:]
    out_a, out_b, bufs, sems = rest
    i, n = pl.program_id(0), pl.num_programs(0)
    tc = x_ref.shape[0]
    slot = i % 2

    def start(step, sl):
        for k in range(TOP_K):
            _start_rows(pos_ref, TOP_K * step * tc + k, TOP_K, tc, o_hbm, bufs.at[sl, k], sems.at[sl, k], chunks)

    @pl.when(i == 0)
    def _():
        start(0, 0)

    @pl.when(i + 1 < n)
    def _():
        start(i + 1, 1 - slot)

    for k in range(TOP_K):
        _wait_rows(tc, o_hbm, bufs.at[slot, k], sems.at[slot, k], chunks)
    w0 = rt_ref[:, TOP_K:TOP_K + 1]
    w1 = rt_ref[:, TOP_K + 1:TOP_K + 2]
    b0, b1 = bufs.at[slot, 0], bufs.at[slot, 1]
    cols = [w0 * b0[pl.ds(s, tc, stride=chunks), :] + w1 * b1[pl.ds(s, tc, stride=chunks), :]
            for s in range(chunks)]
    x = x_ref[...] + g_ref[...] * jnp.concatenate(cols, axis=1)
    y = _rms(x) * n_ref[...]
    if final:
        @pl.when(i < nct)
        def _():
            out_a[...] = y

        @pl.when(i >= nct)
        def _():
            out_b[...] = y
    else:
        out_a[...] = x
        out_b[...] = (y * (1.0 + sc_ref[...]) + sh_ref[...]).astype(BF16)


def _combine(o_tok, pos, route, x, g, n, sc, sh, grp, n_ctx, *, final):
    m, d = x.shape
    chunks = d // LANES
    tc = _tile(math.gcd(m, n_ctx), 256, SUBLANES * DMA_UNROLL)
    nct = n_ctx // tc
    row = pl.BlockSpec((tc, d), lambda i, p: (i, 0))
    gspec = pl.BlockSpec((None, 1, d), lambda i, p: (grp(i), 0, 0))
    args = [pos, o_tok, route, x, g, n.reshape(1, d)]
    specs = [pl.BlockSpec(memory_space=pl.ANY), pl.BlockSpec((tc, LANES), lambda i, p: (i, 0)), row, gspec,
             pl.BlockSpec((1, d), lambda i, p: (0, 0))]
    if not final:
        args += [sc, sh]
        specs += [gspec, gspec]
    if final:
        out_specs = [pl.BlockSpec((tc, d), lambda i, p: (jnp.minimum(i, nct - 1), 0)),
                     pl.BlockSpec((tc, d), lambda i, p: (jnp.maximum(i - nct, 0), 0))]
        out_shape = [jax.ShapeDtypeStruct((n_ctx, d), F32), jax.ShapeDtypeStruct((m - n_ctx, d), F32)]
    else:
        out_specs = [row, row]
        out_shape = [jax.ShapeDtypeStruct((m, d), F32), jax.ShapeDtypeStruct((m, d), BF16)]
    return pl.pallas_call(
        functools.partial(_combine_kernel, chunks=chunks, final=final, nct=nct),
        grid_spec=pltpu.PrefetchScalarGridSpec(
            num_scalar_prefetch=1, grid=(m // tc,), in_specs=specs,
            out_specs=out_specs,
            scratch_shapes=[pltpu.VMEM((2, TOP_K, tc * chunks, LANES), F32),
                            pltpu.SemaphoreType.DMA((2, TOP_K))]),
        out_shape=out_shape,
        compiler_params=_params(("arbitrary",), 2 * TOP_K * tc * d * 4 + 6 * tc * d * 4 + 6 * tc * d * 4),
        name="moe_combine",
    )(*args)


def _route_plan(route, n_exp, tm):
    m = route.shape[0]
    flat_e = route[:, :TOP_K].astype(jnp.int32).reshape(-1)
    onehot = (flat_e[:, None] == jnp.arange(n_exp)[None, :]).astype(jnp.int32)
    csum = jnp.cumsum(onehot, axis=0)
    rank = jnp.take_along_axis(csum, flat_e[:, None], axis=1)[:, 0] - 1
    counts = csum[-1]
    padded = (counts + tm - 1) // tm * tm
    starts = jnp.cumsum(padded) - padded
    pos = starts[flat_e] + rank
    n_rows = TOP_K * m + n_exp * tm
    row_token = jnp.zeros((n_rows,), jnp.int32).at[pos].set(jnp.arange(TOP_K * m, dtype=jnp.int32) // TOP_K)
    tiles = jnp.arange(n_rows // tm, dtype=jnp.int32)
    tile_expert = jnp.sum(tiles[:, None] * tm >= starts[None, :], axis=1).astype(jnp.int32) - 1
    n_used = (jnp.sum(padded) // tm).astype(jnp.int32).reshape(1)
    return pos.astype(jnp.int32), row_token, tile_expert, n_used


def _rope_tables(dec_seq, hd):
    half = hd // 4
    rows = dec_seq // GRID_W
    row_pos = jnp.broadcast_to(jnp.arange(rows)[:, None], (rows, GRID_W)).reshape(-1).astype(F32)
    col_pos = jnp.broadcast_to(jnp.arange(GRID_W)[None, :], (rows, GRID_W)).reshape(-1).astype(F32)
    inv_freq = ROPE_THETA ** (-jnp.arange(0, 2 * half, 2, dtype=F32) / (2 * half))
    ang_r = row_pos[:, None] * inv_freq[None, :]
    ang_c = col_pos[:, None] * inv_freq[None, :]
    cos = jnp.concatenate([jnp.cos(ang_r)] * 2 + [jnp.cos(ang_c)] * 2, axis=1)
    sin = jnp.concatenate([-jnp.sin(ang_r), jnp.sin(ang_r), -jnp.sin(ang_c), jnp.sin(ang_c)], axis=1)
    return cos, sin


def kernel(x_prompt, x_sample, c, cache_k, cache_v, c_ctx, w_mod, b_mod, norm1_g, norm2_g, w_in,
           q_norm_g, k_norm_g, w_pool, pool_scale, w_out, dense_w_gate, dense_w_up, dense_w_down,
           router_w, moe_w_gate, moe_w_up, moe_w_down, final_norm_g):
    batch, seq, d = x_prompt.shape
    dec_batch, dec_seq, _ = x_sample.shape
    depth = w_mod.shape[0]
    n_kv, hd = cache_k.shape[3], cache_k.shape[4]
    kvw = n_kv * hd
    qw = d
    pw = pool_scale.shape[1]
    n_ctx = batch * seq
    m = n_ctx + dec_batch * dec_seq
    dims = dict(n_ctx=n_ctx, seq=seq, dec_seq=dec_seq, qw=qw, kvw=kvw, pw=pw, hd=hd)
    n_exp = router_w.shape[2]
    gw = w_in.shape[2] - qw - 2 * kvw - pw
    assert hd == LANES and d % LANES == 0 and gw == 2 * d

    def grp_for(tile_rows):
        return lambda i: jnp.maximum((i * tile_rows - n_ctx) // dec_seq + 1, 0)

    n_grp = 1 + dec_batch
    rows = -(-n_grp // SUBLANES) * SUBLANES
    cs = jnp.zeros((rows, d), F32).at[0].set(c_ctx).at[1:n_grp].set(c)
    mod = _modulation(cs, w_mod, b_mod)[:, :n_grp]
    mod = mod.reshape(depth, n_grp, N_MOD, 1, d).transpose(0, 2, 1, 3, 4)

    tabs = _rope_tables(dec_seq, hd)
    bands = _pool_bands(min(POOL_BLOCK, seq))
    w_out_b = w_out.astype(BF16)
    moe_w_down_b = moe_w_down.astype(BF16)

    tm_rn = _tile(math.gcd(m, n_ctx), 512, SUBLANES)
    tm_op = _tile(math.gcd(m, n_ctx), 256, SUBLANES)
    x, h = _embed_norm(x_prompt.reshape(n_ctx, d), x_sample.reshape(-1, d), norm1_g[0], mod[0, 1], mod[0, 0],
                       grp_for(tm_rn))

    new_k, new_v = [], []
    for l in range(depth):
        sh1, sc1, g1, sh2, sc2, g2 = (mod[l, k] for k in range(N_MOD))
        q = _proj(h, w_in, l, 0, qw, dims, mode="q", gain=q_norm_g[l], tabs=tabs)
        k, kc = _proj(h, w_in, l, qw, kvw, dims, mode="k", gain=k_norm_g[l], tabs=tabs, with_cache=True)
        v, vc = _proj(h, w_in, l, qw + kvw, kvw, dims, mode="v", with_cache=True)
        u = _proj(h, w_in, l, qw + 2 * kvw, pw, dims, mode="u")
        gates = _proj(h, w_in, l, qw + 2 * kvw + pw, gw, dims, mode="gate")
        new_k.append(kc.reshape(batch, seq, n_kv, hd))
        new_v.append(vc.reshape(batch, seq, n_kv, hd))
        attn_ctx, attn_lat = _attention(q, k, v, cache_k, cache_v, l, dims)
        pool = _pool(u, bands, w_pool, pool_scale[l:l + 1], l, dims)
        last = l == depth - 1
        if last:
            n_next, sc_next, sh_next = final_norm_g, None, None
        else:
            n_next, sc_next, sh_next = norm1_g[l + 1], mod[l + 1, 1], mod[l + 1, 0]
        j = l // 2
        if l % 2 == 0:
            x, h2 = _outproj(attn_ctx, attn_lat, pool, gates, x, w_out_b[l], g1, norm2_g[l], sc2, sh2, grp_for(tm_op), None)
            tm = _tile(m, 1024, SUBLANES)
            nt = jnp.full((1,), m // tm, jnp.int32)
            o = _swiglu(h2, jnp.zeros((m // tm,), jnp.int32), nt, tm,
                        dense_w_gate[:, None], dense_w_up[:, None], dense_w_down[:, None], j, token_major=False)
            x, h = _resid_norm(x, o, g2, n_next, sc_next, sh_next, grp_for(tm_rn), final=last)
        else:
            x, h2_tok, route = _outproj(attn_ctx, attn_lat, pool, gates, x, w_out_b[l], g1, norm2_g[l], sc2, sh2,
                                        grp_for(tm_op), router_w[j])
            tm = _tile(m, 512, SUBLANES)
            pos, row_token, tile_expert, n_used = _route_plan(route, n_exp, tm)
            xs = _gather_rows(h2_tok, row_token, d)
            o_tok = _swiglu(xs, tile_expert, n_used, tm, moe_w_gate, moe_w_up, moe_w_down_b, j, token_major=True)
            x, h = _combine(o_tok, pos, route, x, g2, n_next, sc_next, sh_next, grp_for(tm_op), n_ctx, final=last)
            if last:
                y_ctx, y_lat = x, h

    if depth % 2 == 1:
        y_ctx, y_lat = h[:n_ctx], h[n_ctx:]
    y_prompt = y_ctx.reshape(batch, seq, d)
    y_sample = y_lat.reshape(dec_batch, dec_seq, d)
    return (y_prompt, y_sample, jnp.stack(new_k, axis=1), jnp.stack(new_v, axis=1))
```

```python
import functools
import math

import jax
import jax.numpy as jnp
from jax import lax
from jax.experimental import pallas as pl
from jax.experimental.pallas import tpu as pltpu

GRID_W = 64
ROPE_THETA = 10000.0
POOL_WINDOWS = (2, 4, 8, 16)
TOP_K = 2
N_MOD = 6
EPS = 1e-6

LANES = 128
SUBLANES = 8
VMEM_BUDGET = 60000 * 1024
POOL_BLOCK = 256
ROW_CHUNK = 256
DMA_UNROLL = 8
QK_SCALE_LOG2E = math.log2(math.e)
ATTN_ROWS = 512
PROJ_ROWS = 2048
DENSE_FFN_ROWS = 1024
OUTPROJ_CHUNK = 128
ATTN_LOOKAHEAD = 2

F32 = jnp.float32
BF16 = jnp.bfloat16


def _params(semantics, vmem_bytes):
    limit = int(min(VMEM_BUDGET, vmem_bytes * 5 // 4 + (4 << 20)))
    return pltpu.CompilerParams(dimension_semantics=semantics, vmem_limit_bytes=limit)


def _tile(n, pref, align):
    t = min(n, pref)
    t -= t % align
    while t > align and n % t:
        t -= align
    assert t >= align and n % t == 0, (n, pref, align)
    return t


def _rms(x):
    return x * lax.rsqrt(jnp.mean(x * x, axis=-1, keepdims=True) + EPS)


def _sigmoid(x):
    return 1.0 / (1.0 + jnp.exp(-x))


def _mod_kernel(c_ref, w_ref, b_ref, o_ref):
    c = c_ref[...]
    s = c * _sigmoid(c)
    o_ref[...] = jnp.dot(s, w_ref[...], preferred_element_type=F32,
                         precision=lax.Precision.HIGHEST) + b_ref[...]


def _modulation(cs, w_mod, b_mod):
    depth, d, n = w_mod.shape
    rows = cs.shape[0]
    tn = _tile(n, 1024, LANES)
    return pl.pallas_call(
        _mod_kernel,
        grid=(depth, n // tn),
        in_specs=[pl.BlockSpec((rows, d), lambda l, j: (0, 0)),
                  pl.BlockSpec((None, d, tn), lambda l, j: (l, 0, j)),
                  pl.BlockSpec((None, 1, tn), lambda l, j: (l, 0, j))],
        out_specs=pl.BlockSpec((None, rows, tn), lambda l, j: (l, 0, j)),
        out_shape=jax.ShapeDtypeStruct((depth, rows, n), F32),
        compiler_params=_params(("arbitrary", "arbitrary"), 2 * d * tn * 4 + 4 * rows * (d + tn) * 4),
        name="modulation",
    )(cs, w_mod, b_mod.reshape(depth, 1, n))


def _resid_norm_kernel(x_ref, o_ref, g_ref, n_ref, *rest, final):
    x = x_ref[...] + g_ref[...] * o_ref[...]
    y = _rms(x) * n_ref[...]
    if final:
        (y_ref,) = rest
        y_ref[...] = y
    else:
        sc_ref, sh_ref, xo_ref, h_ref = rest
        xo_ref[...] = x
        h_ref[...] = (y * (1.0 + sc_ref[...]) + sh_ref[...]).astype(BF16)


def _group_spec(d, grp):
    return pl.BlockSpec((None, 1, d), lambda i: (grp(i), 0, 0))


def _resid_norm(x, o, g, n, sc, sh, grp, *, final):
    m, d = x.shape
    tm = _tile(m, 512, SUBLANES)
    row = pl.BlockSpec((tm, d), lambda i: (i, 0))
    args = [x, o, g, n.reshape(1, d)]
    specs = [row, row, _group_spec(d, grp), pl.BlockSpec((1, d), lambda i: (0, 0))]
    if final:
        out_shape, out_specs = [jax.ShapeDtypeStruct((m, d), F32)], [row]
    else:
        args += [sc, sh]
        specs += [_group_spec(d, grp)] * 2
        out_shape = [jax.ShapeDtypeStruct((m, d), F32), jax.ShapeDtypeStruct((m, d), BF16)]
        out_specs = [row, row]
    res = pl.pallas_call(
        functools.partial(_resid_norm_kernel, final=final),
        grid=(m // tm,), in_specs=specs, out_specs=out_specs, out_shape=out_shape,
        compiler_params=_params(("arbitrary",), 8 * tm * d * 4),
        name="resid_norm",
    )(*args)
    return (None, res[0]) if final else res


def _embed_norm_kernel(xp_ref, xs_ref, n_ref, sc_ref, sh_ref, x_ref, h_ref, *, nct):
    x = jnp.where(pl.program_id(0) < nct, xp_ref[...], xs_ref[...])
    x_ref[...] = x
    h_ref[...] = (_rms(x) * n_ref[...] * (1.0 + sc_ref[...]) + sh_ref[...]).astype(BF16)


def _embed_norm(xp, xs, n, sc, sh, grp):
    n_ctx, d = xp.shape
    m = n_ctx + xs.shape[0]
    tm = _tile(math.gcd(m, n_ctx), 512, SUBLANES)
    nct = n_ctx // tm
    row = pl.BlockSpec((tm, d), lambda i: (i, 0))
    return pl.pallas_call(
        functools.partial(_embed_norm_kernel, nct=nct),
        grid=(m // tm,),
        in_specs=[pl.BlockSpec((tm, d), lambda i: (jnp.minimum(i, nct - 1), 0)),
                  pl.BlockSpec((tm, d), lambda i: (jnp.maximum(i - nct, 0), 0)),
                  pl.BlockSpec((1, d), lambda i: (0, 0)), _group_spec(d, grp), _group_spec(d, grp)],
        out_specs=[row, row],
        out_shape=[jax.ShapeDtypeStruct((m, d), F32), jax.ShapeDtypeStruct((m, d), BF16)],
        compiler_params=_params(("arbitrary",), 10 * tm * d * 4),
        name="embed_norm",
    )(xp, xs, n.reshape(1, d), sc, sh)


def _rope(y, cos, sin):
    half = LANES // 4
    lane = lax.broadcasted_iota(jnp.int32, y.shape, 1)
    partner = jnp.where((lane & (2 * half - 1)) < half,
                        pltpu.roll(y, LANES - half, axis=1), pltpu.roll(y, half, axis=1))
    return y * cos + partner * sin


def _cast_weights_once(w_ref, wb):
    @pl.when(pl.program_id(1) == 0)
    def _():
        wb[...] = w_ref[...].astype(BF16)


def _proj_chunks(h_ref, wb, epilogue):
    w = wb[...]
    rows = h_ref.shape[0]
    rsub = _tile(rows, ROW_CHUNK, SUBLANES)
    for rc in range(rows // rsub):
        sl = slice(rc * rsub, (rc + 1) * rsub)
        epilogue(jnp.dot(h_ref[sl, :], w, preferred_element_type=F32), sl)


def _qk_proj_kernel(h_ref, w_ref, g_ref, cos_ref, sin_ref, o_ref, *rest, nct, hd, mult):
    cache, wb = rest[:-1], rest[-1]
    _cast_weights_once(w_ref, wb)

    def run(is_ctx):
        def epilogue(acc, sl):
            cols = []
            for hh in range(acc.shape[1] // hd):
                y = _rms(acc[:, hh * hd:(hh + 1) * hd]) * g_ref[...]
                if not is_ctx:
                    y = _rope(y, cos_ref[sl, :], sin_ref[sl, :])
                cols.append(y)
            y = jnp.concatenate(cols, axis=1) if len(cols) > 1 else cols[0]
            if is_ctx and cache:
                cache[0][sl, :] = y
            o_ref[sl, :] = (y * mult if mult != 1.0 else y).astype(BF16)
        _proj_chunks(h_ref, wb, epilogue)

    i = pl.program_id(1)
    pl.when(i < nct)(lambda: run(True))
    pl.when(i >= nct)(lambda: run(False))


def _plain_proj_kernel(h_ref, w_ref, o_ref, *rest, nct, gate):
    cache, wb = rest[:-1], rest[-1]
    _cast_weights_once(w_ref, wb)

    def run(to_cache):
        def epilogue(acc, sl):
            o_ref[sl, :] = (_sigmoid(acc) if gate else acc).astype(BF16)
            if to_cache:
                cache[0][sl, :] = acc
        _proj_chunks(h_ref, wb, epilogue)

    if cache:
        i = pl.program_id(1)
        pl.when(i < nct)(lambda: run(True))
        pl.when(i >= nct)(lambda: run(False))
    else:
        run(False)


def _proj(h, w_in, layer, col0, width, dims, *, mode, gain=None, tabs=None, with_cache=False):
    m, d = h.shape
    n_ctx, dec_seq, hd = dims["n_ctx"], dims["dec_seq"], dims["hd"]
    tn = _tile(math.gcd(width, col0), 512, LANES)
    tm = _tile(dec_seq, PROJ_ROWS, SUBLANES)
    assert n_ctx % tm == 0 and col0 % tn == 0 and width % tn == 0
    nct, tps = n_ctx // tm, dec_seq // tm
    args = [h, w_in]
    specs = [pl.BlockSpec((tm, d), lambda j, i: (i, 0)),
             pl.BlockSpec((None, d, tn), lambda j, i: (layer, 0, col0 // tn + j))]
    if mode in ("q", "k"):
        tab = pl.BlockSpec((tm, hd), lambda j, i: (jnp.maximum(i - nct, 0) % tps, 0))
        args += [gain.reshape(1, hd), tabs[0], tabs[1]]
        specs += [pl.BlockSpec((1, hd), lambda j, i: (0, 0)), tab, tab]
        body = functools.partial(_qk_proj_kernel, nct=nct, hd=hd,
                                 mult=QK_SCALE_LOG2E * float(hd) ** -0.5 if mode == "q" else 1.0)
    else:
        body = functools.partial(_plain_proj_kernel, nct=nct, gate=mode == "gate")
    out_shape = [jax.ShapeDtypeStruct((m, width), BF16)]
    out_specs = [pl.BlockSpec((tm, tn), lambda j, i: (i, j))]
    if with_cache:
        out_shape.append(jax.ShapeDtypeStruct((n_ctx, width), F32))
        out_specs.append(pl.BlockSpec((tm, tn), lambda j, i: (jnp.minimum(i, nct - 1), j)))
    vmem = (2 * (tm * d * 2 + d * tn * 4 + 2 * tm * hd * 4 + tm * tn * 2 + tm * tn * 4)
            + d * tn * 2 + 8 * ROW_CHUNK * tn * 4)
    res = pl.pallas_call(
        body, grid=(width // tn, m // tm), in_specs=specs, out_specs=out_specs, out_shape=out_shape,
        scratch_shapes=[pltpu.VMEM((d, tn), BF16)],
        compiler_params=_params(("arbitrary", "arbitrary"), vmem), name="proj_" + mode,
    )(*args)
    return res if with_cache else res[0]


def _qk(q, k):
    return lax.dot_general(q, k, (((1,), (1,)), ((), ())), preferred_element_type=F32)


def _attn_ctx_kernel(q_ref, k_ref, v_ref, o_ref, *, n_kv, n_g, hd):
    for kv in range(n_kv):
        k = k_ref[:, kv * hd:(kv + 1) * hd]
        v = v_ref[:, kv * hd:(kv + 1) * hd]
        for g in range(n_g):
            c0 = (kv * n_g + g) * hd
            s = _qk(q_ref[:, c0:c0 + hd], k)
            p = jnp.exp2(s - jnp.max(s, axis=-1, keepdims=True))
            o = jnp.dot(p.astype(BF16), v, preferred_element_type=F32)
            o_ref[:, c0:c0 + hd] = (o / jnp.sum(p, axis=-1, keepdims=True)).astype(BF16)


def _attn_lat_kernel(q_ref, k_ref, v_ref, ck_ref, cv_ref, o_ref, *, n_g, hd):
    k, v = k_ref[...], v_ref[...]
    ck, cv = ck_ref[...].astype(BF16), cv_ref[...].astype(BF16)
    tdot = lambda a, b: lax.dot_general(a, b, (((0,), (0,)), ((), ())), preferred_element_type=F32)

    def scores(g):
        q = q_ref[:, g * hd:(g + 1) * hd]
        return _qk(k, q), _qk(ck, q)

    ahead = [scores(g) for g in range(min(ATTN_LOOKAHEAD, n_g))]
    for g in range(n_g):
        s1, s2 = ahead.pop(0)
        mx = jnp.maximum(jnp.max(s1, axis=0, keepdims=True), jnp.max(s2, axis=0, keepdims=True))
        p1, p2 = jnp.exp2(s1 - mx), jnp.exp2(s2 - mx)
        den = jnp.sum(p1, axis=0, keepdims=True) + jnp.sum(p2, axis=0, keepdims=True)
        if g + ATTN_LOOKAHEAD < n_g:
            ahead.append(scores(g + ATTN_LOOKAHEAD))
        o = (tdot(v, p1.astype(BF16)) + tdot(cv, p2.astype(BF16))) / den
        o_ref[:, g * hd:(g + 1) * hd] = o.T.astype(BF16)


def _attention(q, k, v, cache_k, cache_v, layer, dims):
    m, qw = q.shape
    n_ctx, seq, dec_seq, kvw, hd = (dims[key] for key in ("n_ctx", "seq", "dec_seq", "kvw", "hd"))
    n_kv = kvw // hd
    n_g = qw // kvw
    dec_batch, _, past = cache_k.shape[:3]
    ck = cache_k.reshape(dec_batch, cache_k.shape[1], past, kvw)
    cv = cache_v.reshape(dec_batch, cache_v.shape[1], past, kvw)

    ctx = pl.pallas_call(
        functools.partial(_attn_ctx_kernel, n_kv=n_kv, n_g=n_g, hd=hd),
        grid=(n_ctx // seq,),
        in_specs=[pl.BlockSpec((seq, qw), lambda b: (b, 0)),
                  pl.BlockSpec((seq, kvw), lambda b: (b, 0)),
                  pl.BlockSpec((seq, kvw), lambda b: (b, 0))],
        out_specs=pl.BlockSpec((seq, qw), lambda b: (b, 0)),
        out_shape=jax.ShapeDtypeStruct((n_ctx, qw), BF16),
        compiler_params=_params(("arbitrary",), 4 * seq * (qw + kvw) * 2 + 8 * seq * seq * 4),
        name="attn_ctx",
    )(q, k, v)

    assert n_ctx % dec_seq == 0
    tq = _tile(dec_seq, ATTN_ROWS, SUBLANES)
    gw = n_g * hd
    qrow = lambda b, kv, t: ((n_ctx + b * dec_seq) // tq + t, kv)
    krow = lambda b, kv, t: (n_ctx // dec_seq + b, kv)
    crow = lambda b, kv, t: (b, layer, 0, kv)
    lat = pl.pallas_call(
        functools.partial(_attn_lat_kernel, n_g=n_g, hd=hd),
        grid=(dec_batch, n_kv, dec_seq // tq),
        in_specs=[pl.BlockSpec((tq, gw), qrow),
                  pl.BlockSpec((dec_seq, hd), krow), pl.BlockSpec((dec_seq, hd), krow),
                  pl.BlockSpec((None, None, past, hd), crow), pl.BlockSpec((None, None, past, hd), crow)],
        out_specs=pl.BlockSpec((tq, gw), lambda b, kv, t: (b * (dec_seq // tq) + t, kv)),
        out_shape=jax.ShapeDtypeStruct((m - n_ctx, qw), BF16),
        compiler_params=_params(("arbitrary",) * 3,
                                4 * tq * gw * 2 + 4 * dec_seq * hd * 2 + 4 * past * hd * 4
                                + 6 * tq * (dec_seq + past) * 4),
        name="attn_lat",
    )(q, k, v, ck, cv)
    return ctx, lat


def _pool_kernel(u_ref, band_ref, w_ref, sc_ref, o_ref, *, nct, seq, dec_seq, windows):
    i, g = pl.program_id(0), pl.program_id(1)
    rows = u_ref.shape[0]
    pb = band_ref.shape[-1]
    w = w_ref[...].astype(BF16)
    half = jnp.zeros((), jnp.int32)
    for gi, win in enumerate(windows):
        half = jnp.where(g == gi, win // 2, half)

    def window_sum(b, length):
        p0 = (b * pb) % length
        tot = jnp.dot(band_ref[1], u_ref[b * pb:(b + 1) * pb, :], preferred_element_type=F32)
        if p0 > 0:
            tot += jnp.dot(band_ref[0], u_ref[(b - 1) * pb:b * pb, :], preferred_element_type=F32)
        if p0 + pb < length:
            tot += jnp.dot(band_ref[2], u_ref[(b + 1) * pb:(b + 2) * pb, :], preferred_element_type=F32)
        return tot

    def run(length):
        nb = rows // pb
        tots = [window_sum(0, length)]
        for b in range(nb):
            if b + 1 < nb:
                tots.append(window_sum(b + 1, length))
            p0 = (b * pb) % length
            pos = p0 + lax.broadcasted_iota(jnp.int32, (pb, 1), 0)
            cnt = jnp.minimum(pos + half, length) - jnp.maximum(pos - half, 0)
            pooled = tots[b] / cnt.astype(F32) - u_ref[b * pb:(b + 1) * pb, :].astype(F32)
            mixed = jnp.dot(pooled.astype(BF16), w, preferred_element_type=F32)
            o_ref[b * pb:(b + 1) * pb, :] = (mixed * sc_ref[...]).astype(BF16)

    @pl.when(i < nct)
    def _():
        run(seq)

    @pl.when(i >= nct)
    def _():
        run(dec_seq)


def _pool_bands(pb):
    t = jnp.arange(pb)[:, None]
    j = jnp.arange(pb)[None, :]
    out = []
    for win in POOL_WINDOWS:
        h = win // 2
        prev = (j - pb - t) >= -h
        cur = ((j - t) >= -h) & ((j - t) <= h - 1)
        nxt = (j + pb - t) <= h - 1
        out.append(jnp.stack([prev, cur, nxt]))
    return jnp.stack(out).astype(BF16)


def _pool(u, bands, w_pool, pool_scale, layer, dims):
    m, pw = u.shape
    n_ctx, seq, dec_seq = dims["n_ctx"], dims["seq"], dims["dec_seq"]
    ng = len(POOL_WINDOWS)
    gd = pw // ng
    pb = bands.shape[-1]
    tr = dec_seq
    assert n_ctx % tr == 0 and tr % seq == 0 and seq % pb == 0 and max(POOL_WINDOWS) // 2 <= pb
    return pl.pallas_call(
        functools.partial(_pool_kernel, nct=n_ctx // tr, seq=seq, dec_seq=dec_seq, windows=POOL_WINDOWS),
        grid=(m // tr, ng),
        in_specs=[pl.BlockSpec((tr, gd), lambda i, g: (i, g)),
                  pl.BlockSpec((None, 3, pb, pb), lambda i, g: (g, 0, 0, 0)),
                  pl.BlockSpec((None, None, gd, gd), lambda i, g: (layer, g, 0, 0)),
                  pl.BlockSpec((1, gd), lambda i, g: (0, g))],
        out_specs=pl.BlockSpec((tr, gd), lambda i, g: (i, g)),
        out_shape=jax.ShapeDtypeStruct((m, pw), BF16),
        compiler_params=_params(("arbitrary", "arbitrary"),
                                4 * tr * gd * 2 + 2 * gd * gd * 4 + 6 * pb * pb * 2 + 8 * pb * gd * 4),
        name="pool_mix",
    )(u, bands, w_pool, pool_scale.reshape(1, pw))


def _outproj_kernel(*refs, n_exp, nct):
    (ac_ref, al_ref, p_ref, ga_ref, gb_ref, x_ref, w_ref, g1_ref, n2_ref, sc_ref, sh_ref) = refs[:11]
    rest = refs[11:]
    rows, d = x_ref.shape
    rsub = _tile(rows, OUTPROJ_CHUNK, SUBLANES) if n_exp else rows
    is_ctx = pl.program_id(0) < nct
    w = w_ref[...]
    xs, h2s = [], []
    for r0 in range(0, rows, rsub):
        sl = slice(r0, r0 + rsub)
        attn = jnp.where(is_ctx, ac_ref[sl, :], al_ref[sl, :])
        merged = (ga_ref[sl, :].astype(F32) * attn.astype(F32)
                  + gb_ref[sl, :].astype(F32) * p_ref[sl, :].astype(F32)).astype(BF16)
        x = x_ref[sl, :] + g1_ref[...] * jnp.dot(merged, w, preferred_element_type=F32)
        xs.append(x)
        h2s.append(_rms(x) * n2_ref[...] * (1.0 + sc_ref[...]) + sh_ref[...])
    if n_exp == 0:
        xo_ref, h_ref = rest
        for c, r0 in enumerate(range(0, rows, rsub)):
            xo_ref[r0:r0 + rsub, :] = xs[c]
            h_ref[r0:r0 + rsub, :] = h2s[c].astype(BF16)
        return
    rw_ref, xo_ref, h_ref, rt_ref = rest
    for c, r0 in enumerate(range(0, rows, rsub)):
        xo_ref[r0:r0 + rsub, :] = xs[c]
    h2 = jnp.concatenate(h2s, axis=0) if len(h2s) > 1 else h2s[0]
    chunks = d // LANES
    for s in range(chunks):
        h_ref[pl.ds(s, rows, stride=chunks), :] = h2[:, s * LANES:(s + 1) * LANES]
    h_hi = h2.astype(BF16)
    h_lo = (h2 - h_hi.astype(F32)).astype(BF16)
    part = (jnp.dot(h_hi, rw_ref[...], preferred_element_type=F32)
            + jnp.dot(h_lo, rw_ref[...], preferred_element_type=F32))
    logits = part[:, :LANES] + part[:, LANES:]
    lane = lax.broadcasted_iota(jnp.int32, logits.shape, 1)
    lane_f = lane.astype(F32)
    neg = jnp.float32(-jnp.inf)
    lg = jnp.where(lane < n_exp, logits, neg)
    m1 = jnp.max(lg, axis=-1, keepdims=True)
    i1 = jnp.min(jnp.where(lg == m1, lane_f, float(LANES)), axis=-1, keepdims=True)
    lg2 = jnp.where(lane_f == i1, neg, lg)
    m2 = jnp.max(lg2, axis=-1, keepdims=True)
    i2 = jnp.min(jnp.where(lg2 == m2, lane_f, float(LANES)), axis=-1, keepdims=True)
    e2 = jnp.exp(m2 - m1)
    den = 1.0 + e2
    rt = jnp.where(lane == 0, i1, 0.0)
    rt = jnp.where(lane == 1, i2, rt)
    rt = jnp.where(lane == 2, 1.0 / den, rt)
    rt_ref[...] = jnp.where(lane == 3, e2 / den, rt)


def _outproj(attn_ctx, attn_lat, pool, gates, x, w_out_b, g1, n2, sc2, sh2, grp, router_w):
    m, d = x.shape
    tm = _tile(math.gcd(m, attn_ctx.shape[0]), 256, SUBLANES)
    nct = attn_ctx.shape[0] // tm
    chunks = d // LANES
    row = pl.BlockSpec((tm, d), lambda i: (i, 0))
    vec = pl.BlockSpec((1, d), lambda i: (0, 0))
    args = [attn_ctx, attn_lat, pool, gates, gates, x, w_out_b, g1, n2.reshape(1, d), sc2, sh2]
    specs = [pl.BlockSpec((tm, d), lambda i: (jnp.minimum(i, nct - 1), 0)),
             pl.BlockSpec((tm, d), lambda i: (jnp.maximum(i - nct, 0), 0)),
             row, row, pl.BlockSpec((tm, d), lambda i: (i, 1)), row,
             pl.BlockSpec((d, d), lambda i: (0, 0)), _group_spec(d, grp), vec,
             _group_spec(d, grp), _group_spec(d, grp)]
    out_shape = [jax.ShapeDtypeStruct((m, d), F32)]
    out_specs = [row]
    n_exp = 0
    if router_w is None:
        out_shape.append(jax.ShapeDtypeStruct((m, d), BF16))
        out_specs.append(row)
    else:
        n_exp = router_w.shape[1]
        assert TOP_K == 2 and 2 * TOP_K <= LANES and n_exp <= LANES
        rw = jnp.pad(router_w, ((0, 0), (0, LANES - n_exp)))
        rw_hi = rw.astype(BF16)
        rw_lo = (rw - rw_hi.astype(F32)).astype(BF16)
        args.append(jnp.concatenate([rw_hi, rw_lo], axis=1))
        specs.append(pl.BlockSpec((d, 2 * LANES), lambda i: (0, 0)))
        out_shape += [jax.ShapeDtypeStruct((m * chunks, LANES), F32), jax.ShapeDtypeStruct((m, LANES), F32)]
        out_specs += [pl.BlockSpec((tm * chunks, LANES), lambda i: (i, 0)),
                      pl.BlockSpec((tm, LANES), lambda i: (i, 0))]
    return pl.pallas_call(
        functools.partial(_outproj_kernel, n_exp=n_exp, nct=nct),
        grid=(m // tm,), in_specs=specs, out_specs=out_specs, out_shape=out_shape,
        compiler_params=_params(("arbitrary",), 2 * tm * d * (4 * 2 + 3 * 4) + 2 * d * d * 2 + 8 * tm * d * 4),
        name="outproj_route" if n_exp else "outproj",
    )(*args)


def _changed(te_ref, i):
    return (i == 0) | (te_ref[i] != te_ref[jnp.maximum(i - 1, 0)])


def _up_kernel(te_ref, nu_ref, x_ref, wg_ref, wu_ref, o_ref, wgb, wub):
    i = pl.program_id(1)

    @pl.when(_changed(te_ref, i))
    def _():
        wgb[...] = wg_ref[...].astype(BF16)
        wub[...] = wu_ref[...].astype(BF16)

    @pl.when(i < nu_ref[0])
    def _():
        x = x_ref[...]
        a = jnp.dot(x, wgb[...], preferred_element_type=F32)
        b = jnp.dot(x, wub[...], preferred_element_type=F32)
        o_ref[...] = (a * _sigmoid(a) * b).astype(BF16)

    @pl.when(i >= nu_ref[0])
    def _():
        o_ref[...] = jnp.zeros(o_ref.shape, BF16)


def _down_kernel(te_ref, nu_ref, a_ref, w_ref, o_ref, wb):
    i = pl.program_id(1)

    @pl.when(_changed(te_ref, i))
    def _():
        wb[...] = w_ref[...].astype(BF16)

    o_ref[...] = jnp.dot(a_ref[...], wb[...], preferred_element_type=F32)


def _down_tok_kernel(te_ref, nu_ref, a_ref, w_ref, o_ref, *, chunks):
    i, j = pl.program_id(0), pl.program_id(1)
    tm = a_ref.shape[0]
    per = w_ref.shape[1] // LANES

    @pl.when(i < nu_ref[0])
    def _():
        acc = jnp.dot(a_ref[...], w_ref[...], preferred_element_type=F32)
        for jj in range(chunks // per):
            @pl.when(j == jj)
            def _():
                for c in range(per):
                    o_ref[pl.ds(jj * per + c, tm, stride=chunks), :] = acc[:, c * LANES:(c + 1) * LANES]

    @pl.when((i >= nu_ref[0]) & (j == 0))
    def _():
        o_ref[...] = jnp.zeros(o_ref.shape, F32)


def _swiglu(xs, tile_expert, n_used, tm, w_gate, w_up, w_down, sl, *, token_major):
    r, d = xs.shape
    f = w_gate.shape[3]
    tf = _tile(f, 512, LANES)
    tn = _tile(d, 512, LANES)
    nt = r // tm
    used = lambda i, nu: jnp.minimum(i, nu[0] - 1)
    act = pl.pallas_call(
        _up_kernel,
        grid_spec=pltpu.PrefetchScalarGridSpec(
            num_scalar_prefetch=2, grid=(f // tf, nt),
            in_specs=[pl.BlockSpec((tm, d), lambda j, i, te, nu: (used(i, nu), 0)),
                      pl.BlockSpec((None, None, d, tf), lambda j, i, te, nu: (sl, te[i], 0, j)),
                      pl.BlockSpec((None, None, d, tf), lambda j, i, te, nu: (sl, te[i], 0, j))],
            out_specs=pl.BlockSpec((tm, tf), lambda j, i, te, nu: (i, j)),
            scratch_shapes=[pltpu.VMEM((d, tf), BF16), pltpu.VMEM((d, tf), BF16)]),
        out_shape=jax.ShapeDtypeStruct((r, f), BF16),
        compiler_params=_params(("arbitrary", "arbitrary"),
                                2 * (tm * d * 2 + 2 * d * tf * 4 + tm * tf * 2) + 2 * d * tf * 2 + 4 * tm * tf * 4),
        name="swiglu_up",
    )(tile_expert, n_used, xs, w_gate, w_up)
    if not token_major:
        return pl.pallas_call(
            _down_kernel,
            grid_spec=pltpu.PrefetchScalarGridSpec(
                num_scalar_prefetch=2, grid=(d // tn, nt),
                in_specs=[pl.BlockSpec((tm, f), lambda j, i, te, nu: (i, 0)),
                          pl.BlockSpec((None, None, f, tn), lambda j, i, te, nu: (sl, te[i], 0, j))],
                out_specs=pl.BlockSpec((tm, tn), lambda j, i, te, nu: (i, j)),
                scratch_shapes=[pltpu.VMEM((f, tn), BF16)]),
            out_shape=jax.ShapeDtypeStruct((r, d), F32),
            compiler_params=_params(("arbitrary", "arbitrary"),
                                    2 * (tm * f * 2 + f * tn * 4 + tm * tn * 4) + f * tn * 2 + 2 * tm * tn * 4),
            name="swiglu_down",
        )(tile_expert, n_used, act, w_down)
    chunks = d // LANES
    tn = _tile(d, 1024, LANES)
    n_j = d // tn
    col = lambda i, j, nu: jnp.where(i < nu[0], j, n_j - 1)
    return pl.pallas_call(
        functools.partial(_down_tok_kernel, chunks=chunks),
        grid_spec=pltpu.PrefetchScalarGridSpec(
            num_scalar_prefetch=2, grid=(nt, n_j),
            in_specs=[pl.BlockSpec((tm, f), lambda i, j, te, nu: (used(i, nu), 0)),
                      pl.BlockSpec((None, None, f, tn), lambda i, j, te, nu: (sl, te[i], 0, col(i, j, nu)))],
            out_specs=pl.BlockSpec((tm * chunks, LANES), lambda i, j, te, nu: (i, 0))),
        out_shape=jax.ShapeDtypeStruct((r * chunks, LANES), F32),
        compiler_params=_params(("arbitrary", "arbitrary"),
                                2 * (tm * f * 2 + f * tn * 2 + tm * d * 4) + 4 * tm * tn * 4),
        name="swiglu_down_tok",
    )(tile_expert, n_used, act, w_down)


def _row_copy(src_hbm, src_row, dst, dst_row, sem, chunks):
    return pltpu.make_async_copy(
        src_hbm.at[pl.ds(pl.multiple_of(src_row * chunks, chunks), chunks), :],
        dst.at[pl.ds(pl.multiple_of(dst_row * chunks, chunks), chunks), :], sem)


def _start_rows(idx_ref, base, stride, n, src_hbm, dst, sem, chunks):
    def body(o, c):
        for u in range(DMA_UNROLL):
            r = o * DMA_UNROLL + u
            _row_copy(src_hbm, idx_ref[base + stride * r], dst, r, sem, chunks).start(priority=u % 2)
        return c
    lax.fori_loop(0, n // DMA_UNROLL, body, 0)


def _wait_rows(n, src_hbm, dst, sem, chunks):
    def body(o, c):
        for u in range(DMA_UNROLL):
            _row_copy(src_hbm, 0, dst, o * DMA_UNROLL + u, sem, chunks).wait()
        return c
    lax.fori_loop(0, n // DMA_UNROLL, body, 0)


def _gather_kernel(rt_ref, h_hbm, o_ref, buf, sems, *, chunks):
    i, n = pl.program_id(0), pl.num_programs(0)
    tg = o_ref.shape[0]
    slot = i % 2

    @pl.when(i == 0)
    def _():
        _start_rows(rt_ref, 0, 1, tg, h_hbm, buf.at[0], sems.at[0], chunks)

    @pl.when(i + 1 < n)
    def _():
        _start_rows(rt_ref, (i + 1) * tg, 1, tg, h_hbm, buf.at[1 - slot], sems.at[1 - slot], chunks)

    cur = buf.at[slot]
    _wait_rows(tg, h_hbm, cur, sems.at[slot], chunks)
    for s in range(chunks):
        o_ref[:, s * LANES:(s + 1) * LANES] = cur[pl.ds(s, tg, stride=chunks), :].astype(BF16)


def _gather_rows(h_tok, row_token, d):
    chunks = d // LANES
    r = row_token.shape[0]
    tg = _tile(r, 256, SUBLANES * DMA_UNROLL)
    return pl.pallas_call(
        functools.partial(_gather_kernel, chunks=chunks),
        grid_spec=pltpu.PrefetchScalarGridSpec(
            num_scalar_prefetch=1, grid=(r // tg,),
            in_specs=[pl.BlockSpec(memory_space=pl.ANY)],
            out_specs=pl.BlockSpec((tg, d), lambda i, rt: (i, 0)),
            scratch_shapes=[pltpu.VMEM((2, tg * chunks, LANES), F32), pltpu.SemaphoreType.DMA((2,))]),
        out_shape=jax.ShapeDtypeStruct((r, d), BF16),
        compiler_params=_params(("arbitrary",), 2 * tg * d * 4 + 2 * tg * d * 2 + 2 * tg * d * 4),
        name="moe_gather",
    )(row_token, h_tok)


def _combine_kernel(*refs, chunks, final, nct):
    pos_ref, o_hbm, rt_ref, x_ref, g_ref, n_ref = refs[:6]
    rest = list(refs[6:])
    if not final:
        sc_ref, sh_ref = rest[:2]
        rest = rest[2:]
    out_a, out_b, bufs, sems = rest
    i, n = pl.program_id(0), pl.num_programs(0)
    tc = x_ref.shape[0]
    slot = i % 2

    def start(step, sl):
        for k in range(TOP_K):
            _start_rows(pos_ref, TOP_K * step * tc + k, TOP_K, tc, o_hbm, bufs.at[sl, k], sems.at[sl, k], chunks)

    @pl.when(i == 0)
    def _():
        start(0, 0)

    @pl.when(i + 1 < n)
    def _():
        start(i + 1, 1 - slot)

    for k in range(TOP_K):
        _wait_rows(tc, o_hbm, bufs.at[slot, k], sems.at[slot, k], chunks)
    w0 = rt_ref[:, TOP_K:TOP_K + 1]
    w1 = rt_ref[:, TOP_K + 1:TOP_K + 2]
    b0, b1 = bufs.at[slot, 0], bufs.at[slot, 1]
    cols = [w0 * b0[pl.ds(s, tc, stride=chunks), :] + w1 * b1[pl.ds(s, tc, stride=chunks), :]
            for s in range(chunks)]
    x = x_ref[...] + g_ref[...] * jnp.concatenate(cols, axis=1)
    y = _rms(x) * n_ref[...]
    if final:
        @pl.when(i < nct)
        def _():
            out_a[...] = y

        @pl.when(i >= nct)
        def _():
            out_b[...] = y
    else:
        out_a[...] = x
        out_b[...] = (y * (1.0 + sc_ref[...]) + sh_ref[...]).astype(BF16)


def _combine(o_tok, pos, route, x, g, n, sc, sh, grp, n_ctx, *, final):
    m, d = x.shape
    chunks = d // LANES
    tc = _tile(math.gcd(m, n_ctx), 256, SUBLANES * DMA_UNROLL)
    nct = n_ctx // tc
    row = pl.BlockSpec((tc, d), lambda i, p: (i, 0))
    gspec = pl.BlockSpec((None, 1, d), lambda i, p: (grp(i), 0, 0))
    args = [pos, o_tok, route, x, g, n.reshape(1, d)]
    specs = [pl.BlockSpec(memory_space=pl.ANY), pl.BlockSpec((tc, LANES), lambda i, p: (i, 0)), row, gspec,
             pl.BlockSpec((1, d), lambda i, p: (0, 0))]
    if not final:
        args += [sc, sh]
        specs += [gspec, gspec]
    if final:
        out_specs = [pl.BlockSpec((tc, d), lambda i, p: (jnp.minimum(i, nct - 1), 0)),
                     pl.BlockSpec((tc, d), lambda i, p: (jnp.maximum(i - nct, 0), 0))]
        out_shape = [jax.ShapeDtypeStruct((n_ctx, d), F32), jax.ShapeDtypeStruct((m - n_ctx, d), F32)]
    else:
        out_specs = [row, row]
        out_shape = [jax.ShapeDtypeStruct((m, d), F32), jax.ShapeDtypeStruct((m, d), BF16)]
    return pl.pallas_call(
        functools.partial(_combine_kernel, chunks=chunks, final=final, nct=nct),
        grid_spec=pltpu.PrefetchScalarGridSpec(
            num_scalar_prefetch=1, grid=(m // tc,), in_specs=specs,
            out_specs=out_specs,
            scratch_shapes=[pltpu.VMEM((2, TOP_K, tc * chunks, LANES), F32),
                            pltpu.SemaphoreType.DMA((2, TOP_K))]),
        out_shape=out_shape,
        compiler_params=_params(("arbitrary",), 2 * TOP_K * tc * d * 4 + 6 * tc * d * 4 + 6 * tc * d * 4),
        name="moe_combine",
    )(*args)


def _route_plan(route, n_exp, tm):
    m = route.shape[0]
    flat_e = route[:, :TOP_K].astype(jnp.int32).reshape(-1)
    onehot = (flat_e[:, None] == jnp.arange(n_exp)[None, :]).astype(jnp.int32)
    csum = jnp.cumsum(onehot, axis=0)
    rank = jnp.take_along_axis(csum, flat_e[:, None], axis=1)[:, 0] - 1
    counts = csum[-1]
    padded = (counts + tm - 1) // tm * tm
    starts = jnp.cumsum(padded) - padded
    pos = starts[flat_e] + rank
    n_rows = TOP_K * m + n_exp * tm
    row_token = jnp.zeros((n_rows,), jnp.int32).at[pos].set(jnp.arange(TOP_K * m, dtype=jnp.int32) // TOP_K)
    tiles = jnp.arange(n_rows // tm, dtype=jnp.int32)
    tile_expert = jnp.sum(tiles[:, None] * tm >= starts[None, :], axis=1).astype(jnp.int32) - 1
    n_used = (jnp.sum(padded) // tm).astype(jnp.int32).reshape(1)
    return pos.astype(jnp.int32), row_token, tile_expert, n_used


def _rope_tables(dec_seq, hd):
    half = hd // 4
    rows = dec_seq // GRID_W
    row_pos = jnp.broadcast_to(jnp.arange(rows)[:, None], (rows, GRID_W)).reshape(-1).astype(F32)
    col_pos = jnp.broadcast_to(jnp.arange(GRID_W)[None, :], (rows, GRID_W)).reshape(-1).astype(F32)
    inv_freq = ROPE_THETA ** (-jnp.arange(0, 2 * half, 2, dtype=F32) / (2 * half))
    ang_r = row_pos[:, None] * inv_freq[None, :]
    ang_c = col_pos[:, None] * inv_freq[None, :]
    cos = jnp.concatenate([jnp.cos(ang_r)] * 2 + [jnp.cos(ang_c)] * 2, axis=1)
    sin = jnp.concatenate([-jnp.sin(ang_r), jnp.sin(ang_r), -jnp.sin(ang_c), jnp.sin(ang_c)], axis=1)
    return cos, sin


def kernel(x_prompt, x_sample, c, cache_k, cache_v, c_ctx, w_mod, b_mod, norm1_g, norm2_g, w_in,
           q_norm_g, k_norm_g, w_pool, pool_scale, w_out, dense_w_gate, dense_w_up, dense_w_down,
           router_w, moe_w_gate, moe_w_up, moe_w_down, final_norm_g):
    batch, seq, d = x_prompt.shape
    dec_batch, dec_seq, _ = x_sample.shape
    depth = w_mod.shape[0]
    n_kv, hd = cache_k.shape[3], cache_k.shape[4]
    kvw = n_kv * hd
    qw = d
    pw = pool_scale.shape[1]
    n_ctx = batch * seq
    m = n_ctx + dec_batch * dec_seq
    dims = dict(n_ctx=n_ctx, seq=seq, dec_seq=dec_seq, qw=qw, kvw=kvw, pw=pw, hd=hd)
    n_exp = router_w.shape[2]
    gw = w_in.shape[2] - qw - 2 * kvw - pw
    assert hd == LANES and d % LANES == 0 and gw == 2 * d

    def grp_for(tile_rows):
        return lambda i: jnp.maximum((i * tile_rows - n_ctx) // dec_seq + 1, 0)

    n_grp = 1 + dec_batch
    rows = -(-n_grp // SUBLANES) * SUBLANES
    cs = jnp.zeros((rows, d), F32).at[0].set(c_ctx).at[1:n_grp].set(c)
    mod = _modulation(cs, w_mod, b_mod)[:, :n_grp]
    mod = mod.reshape(depth, n_grp, N_MOD, 1, d).transpose(0, 2, 1, 3, 4)

    tabs = _rope_tables(dec_seq, hd)
    bands = _pool_bands(min(POOL_BLOCK, seq))
    w_out_b = w_out.astype(BF16)
    moe_w_down_b = moe_w_down.astype(BF16)

    tm_rn = _tile(math.gcd(m, n_ctx), 512, SUBLANES)
    tm_op = _tile(math.gcd(m, n_ctx), 256, SUBLANES)
    assert tm_rn == _tile(m, 512, SUBLANES) and dec_seq % tm_rn == 0 and dec_seq % tm_op == 0
    x, h = _embed_norm(x_prompt.reshape(n_ctx, d), x_sample.reshape(-1, d), norm1_g[0], mod[0, 1], mod[0, 0],
                       grp_for(tm_rn))

    new_k, new_v = [], []
    for l in range(depth):
        sh1, sc1, g1, sh2, sc2, g2 = (mod[l, k] for k in range(N_MOD))
        q = _proj(h, w_in, l, 0, qw, dims, mode="q", gain=q_norm_g[l], tabs=tabs)
        k, kc = _proj(h, w_in, l, qw, kvw, dims, mode="k", gain=k_norm_g[l], tabs=tabs, with_cache=True)
        v, vc = _proj(h, w_in, l, qw + kvw, kvw, dims, mode="v", with_cache=True)
        u = _proj(h, w_in, l, qw + 2 * kvw, pw, dims, mode="u")
        gates = _proj(h, w_in, l, qw + 2 * kvw + pw, gw, dims, mode="gate")
        new_k.append(kc.reshape(batch, seq, n_kv, hd))
        new_v.append(vc.reshape(batch, seq, n_kv, hd))
        attn_ctx, attn_lat = _attention(q, k, v, cache_k, cache_v, l, dims)
        pool = _pool(u, bands, w_pool, pool_scale[l:l + 1], l, dims)
        last = l == depth - 1
        if last:
            n_next, sc_next, sh_next = final_norm_g, None, None
        else:
            n_next, sc_next, sh_next = norm1_g[l + 1], mod[l + 1, 1], mod[l + 1, 0]
        j = l // 2
        if l % 2 == 0:
            x, h2 = _outproj(attn_ctx, attn_lat, pool, gates, x, w_out_b[l], g1, norm2_g[l], sc2, sh2, grp_for(tm_op), None)
            tm = _tile(m, DENSE_FFN_ROWS, SUBLANES)
            nt = jnp.full((1,), m // tm, jnp.int32)
            o = _swiglu(h2, jnp.zeros((m // tm,), jnp.int32), nt, tm,
                        dense_w_gate[:, None], dense_w_up[:, None], dense_w_down[:, None], j, token_major=False)
            x, h = _resid_norm(x, o, g2, n_next, sc_next, sh_next, grp_for(tm_rn), final=last)
        else:
            x, h2_tok, route = _outproj(attn_ctx, attn_lat, pool, gates, x, w_out_b[l], g1, norm2_g[l], sc2, sh2,
                                        grp_for(tm_op), router_w[j])
            tm = _tile(m, 512, SUBLANES)
            pos, row_token, tile_expert, n_used = _route_plan(route, n_exp, tm)
            xs = _gather_rows(h2_tok, row_token, d)
            o_tok = _swiglu(xs, tile_expert, n_used, tm, moe_w_gate, moe_w_up, moe_w_down_b, j, token_major=True)
            x, h = _combine(o_tok, pos, route, x, g2, n_next, sc_next, sh_next, grp_for(tm_op), n_ctx, final=last)
            if last:
                y_ctx, y_lat = x, h

    if depth % 2 == 1:
        y_ctx, y_lat = h[:n_ctx], h[n_ctx:]
    y_prompt = y_ctx.reshape(batch, seq, d)
    y_sample = y_lat.reshape(dec_batch, dec_seq, d)
    return (y_prompt, y_sample, jnp.stack(new_k, axis=1), jnp.stack(new_v, axis=1))
```

```python
import functools
import math

import jax
import jax.numpy as jnp
from jax import lax
from jax.experimental import pallas as pl
from jax.experimental.pallas import tpu as pltpu

GRID_W = 64
ROPE_THETA = 10000.0
POOL_WINDOWS = (2, 4, 8, 16)
TOP_K = 2
N_MOD = 6
EPS = 1e-6

LANES = 128
SUBLANES = 8
VMEM_BUDGET = 60000 * 1024
POOL_BLOCK = 256
ROW_CHUNK = 256
DMA_UNROLL = 8
QK_SCALE_LOG2E = math.log2(math.e)
ATTN_ROWS = 512
PROJ_ROWS = 2048
DENSE_FFN_ROWS = 1024
OUTPROJ_CHUNK = 128
ATTN_LOOKAHEAD = 2

F32 = jnp.float32
BF16 = jnp.bfloat16


def _params(semantics, vmem_bytes):
    limit = int(min(VMEM_BUDGET, vmem_bytes * 5 // 4 + (4 << 20)))
    return pltpu.CompilerParams(dimension_semantics=semantics, vmem_limit_bytes=limit)


def _tile(n, pref, align):
    t = min(n, pref)
    t -= t % align
    while t > align and n % t:
        t -= align
    assert t >= align and n % t == 0, (n, pref, align)
    return t


def _rms(x):
    return x * lax.rsqrt(jnp.mean(x * x, axis=-1, keepdims=True) + EPS)


def _sigmoid(x):
    return 1.0 / (1.0 + jnp.exp(-x))


def _mod_kernel(c_ref, w_ref, b_ref, o_ref):
    c = c_ref[...]
    s = c * _sigmoid(c)
    w = w_ref[...]
    s_hi, w_hi = s.astype(BF16), w.astype(BF16)
    s_lo = (s - s_hi.astype(F32)).astype(BF16)
    w_lo = (w - w_hi.astype(F32)).astype(BF16)
    dot = functools.partial(jnp.dot, preferred_element_type=F32)
    o_ref[...] = dot(s_hi, w_hi) + dot(s_lo, w_hi) + dot(s_hi, w_lo) + b_ref[...]


def _modulation(cs, w_mod, b_mod):
    depth, d, n = w_mod.shape
    rows = cs.shape[0]
    tn = _tile(n, 1024, LANES)
    return pl.pallas_call(
        _mod_kernel,
        grid=(depth, n // tn),
        in_specs=[pl.BlockSpec((rows, d), lambda l, j: (0, 0)),
                  pl.BlockSpec((None, d, tn), lambda l, j: (l, 0, j)),
                  pl.BlockSpec((None, 1, tn), lambda l, j: (l, 0, j))],
        out_specs=pl.BlockSpec((None, rows, tn), lambda l, j: (l, 0, j)),
        out_shape=jax.ShapeDtypeStruct((depth, rows, n), F32),
        compiler_params=_params(("arbitrary", "arbitrary"), 2 * d * tn * 4 + 4 * rows * (d + tn) * 4),
        name="modulation",
    )(cs, w_mod, b_mod.reshape(depth, 1, n))


def _resid_norm_kernel(x_ref, o_ref, g_ref, n_ref, *rest, final):
    x = x_ref[...] + g_ref[...] * o_ref[...]
    y = _rms(x) * n_ref[...]
    if final:
        (y_ref,) = rest
        y_ref[...] = y
    else:
        sc_ref, sh_ref, xo_ref, h_ref = rest
        xo_ref[...] = x
        h_ref[...] = (y * (1.0 + sc_ref[...]) + sh_ref[...]).astype(BF16)


def _group_spec(d, grp):
    return pl.BlockSpec((None, 1, d), lambda i: (grp(i), 0, 0))


def _resid_norm(x, o, g, n, sc, sh, grp, *, final):
    m, d = x.shape
    tm = _tile(m, 512, SUBLANES)
    row = pl.BlockSpec((tm, d), lambda i: (i, 0))
    args = [x, o, g, n.reshape(1, d)]
    specs = [row, row, _group_spec(d, grp), pl.BlockSpec((1, d), lambda i: (0, 0))]
    if final:
        out_shape, out_specs = [jax.ShapeDtypeStruct((m, d), F32)], [row]
    else:
        args += [sc, sh]
        specs += [_group_spec(d, grp)] * 2
        out_shape = [jax.ShapeDtypeStruct((m, d), F32), jax.ShapeDtypeStruct((m, d), BF16)]
        out_specs = [row, row]
    res = pl.pallas_call(
        functools.partial(_resid_norm_kernel, final=final),
        grid=(m // tm,), in_specs=specs, out_specs=out_specs, out_shape=out_shape,
        compiler_params=_params(("arbitrary",), 8 * tm * d * 4),
        name="resid_norm",
    )(*args)
    return (None, res[0]) if final else res


def _embed_norm_kernel(xp_ref, xs_ref, n_ref, sc_ref, sh_ref, x_ref, h_ref, *, nct):
    x = jnp.where(pl.program_id(0) < nct, xp_ref[...], xs_ref[...])
    x_ref[...] = x
    h_ref[...] = (_rms(x) * n_ref[...] * (1.0 + sc_ref[...]) + sh_ref[...]).astype(BF16)


def _embed_norm(xp, xs, n, sc, sh, grp):
    n_ctx, d = xp.shape
    m = n_ctx + xs.shape[0]
    tm = _tile(math.gcd(m, n_ctx), 512, SUBLANES)
    nct = n_ctx // tm
    row = pl.BlockSpec((tm, d), lambda i: (i, 0))
    return pl.pallas_call(
        functools.partial(_embed_norm_kernel, nct=nct),
        grid=(m // tm,),
        in_specs=[pl.BlockSpec((tm, d), lambda i: (jnp.minimum(i, nct - 1), 0)),
                  pl.BlockSpec((tm, d), lambda i: (jnp.maximum(i - nct, 0), 0)),
                  pl.BlockSpec((1, d), lambda i: (0, 0)), _group_spec(d, grp), _group_spec(d, grp)],
        out_specs=[row, row],
        out_shape=[jax.ShapeDtypeStruct((m, d), F32), jax.ShapeDtypeStruct((m, d), BF16)],
        compiler_params=_params(("arbitrary",), 10 * tm * d * 4),
        name="embed_norm",
    )(xp, xs, n.reshape(1, d), sc, sh)


def _rope(y, cos, sin):
    half = LANES // 4
    lane = lax.broadcasted_iota(jnp.int32, y.shape, 1)
    partner = jnp.where((lane & (2 * half - 1)) < half,
                        pltpu.roll(y, LANES - half, axis=1), pltpu.roll(y, half, axis=1))
    return y * cos + partner * sin


def _cast_weights_once(w_ref, wb):
    @pl.when(pl.program_id(1) == 0)
    def _():
        wb[...] = w_ref[...].astype(BF16)


def _proj_chunks(h_ref, wb, epilogue):
    w = wb[...]
    rows = h_ref.shape[0]
    rsub = _tile(rows, ROW_CHUNK, SUBLANES)
    for rc in range(rows // rsub):
        sl = slice(rc * rsub, (rc + 1) * rsub)
        epilogue(jnp.dot(h_ref[sl, :], w, preferred_element_type=F32), sl)


def _qk_proj_kernel(h_ref, w_ref, g_ref, cos_ref, sin_ref, o_ref, *rest, nct, hd, mult):
    cache, wb = rest[:-1], rest[-1]
    _cast_weights_once(w_ref, wb)

    def run(is_ctx):
        def epilogue(acc, sl):
            cols = []
            for hh in range(acc.shape[1] // hd):
                y = _rms(acc[:, hh * hd:(hh + 1) * hd]) * g_ref[...]
                if not is_ctx:
                    y = _rope(y, cos_ref[sl, :], sin_ref[sl, :])
                cols.append(y)
            y = jnp.concatenate(cols, axis=1) if len(cols) > 1 else cols[0]
            if is_ctx and cache:
                cache[0][sl, :] = y
            o_ref[sl, :] = (y * mult if mult != 1.0 else y).astype(BF16)
        _proj_chunks(h_ref, wb, epilogue)

    i = pl.program_id(1)
    pl.when(i < nct)(lambda: run(True))
    pl.when(i >= nct)(lambda: run(False))


def _plain_proj_kernel(h_ref, w_ref, o_ref, *rest, nct, gate):
    cache, wb = rest[:-1], rest[-1]
    _cast_weights_once(w_ref, wb)

    def run(to_cache):
        def epilogue(acc, sl):
            o_ref[sl, :] = (_sigmoid(acc) if gate else acc).astype(BF16)
            if to_cache:
                cache[0][sl, :] = acc
        _proj_chunks(h_ref, wb, epilogue)

    if cache:
        i = pl.program_id(1)
        pl.when(i < nct)(lambda: run(True))
        pl.when(i >= nct)(lambda: run(False))
    else:
        run(False)


def _proj(h, w_in, layer, col0, width, dims, *, mode, gain=None, tabs=None, with_cache=False):
    m, d = h.shape
    n_ctx, dec_seq, hd = dims["n_ctx"], dims["dec_seq"], dims["hd"]
    tn = _tile(math.gcd(width, col0), 512, LANES)
    tm = _tile(dec_seq, PROJ_ROWS, SUBLANES)
    assert n_ctx % tm == 0 and col0 % tn == 0 and width % tn == 0
    nct, tps = n_ctx // tm, dec_seq // tm
    args = [h, w_in]
    specs = [pl.BlockSpec((tm, d), lambda j, i: (i, 0)),
             pl.BlockSpec((None, d, tn), lambda j, i: (layer, 0, col0 // tn + j))]
    if mode in ("q", "k"):
        tab = pl.BlockSpec((tm, hd), lambda j, i: (jnp.maximum(i - nct, 0) % tps, 0))
        args += [gain.reshape(1, hd), tabs[0], tabs[1]]
        specs += [pl.BlockSpec((1, hd), lambda j, i: (0, 0)), tab, tab]
        body = functools.partial(_qk_proj_kernel, nct=nct, hd=hd,
                                 mult=QK_SCALE_LOG2E * float(hd) ** -0.5 if mode == "q" else 1.0)
    else:
        body = functools.partial(_plain_proj_kernel, nct=nct, gate=mode == "gate")
    out_shape = [jax.ShapeDtypeStruct((m, width), BF16)]
    out_specs = [pl.BlockSpec((tm, tn), lambda j, i: (i, j))]
    if with_cache:
        out_shape.append(jax.ShapeDtypeStruct((n_ctx, width), F32))
        out_specs.append(pl.BlockSpec((tm, tn), lambda j, i: (jnp.minimum(i, nct - 1), j)))
    vmem = (2 * (tm * d * 2 + d * tn * 4 + 2 * tm * hd * 4 + tm * tn * 2 + tm * tn * 4)
            + d * tn * 2 + 8 * ROW_CHUNK * tn * 4)
    res = pl.pallas_call(
        body, grid=(width // tn, m // tm), in_specs=specs, out_specs=out_specs, out_shape=out_shape,
        scratch_shapes=[pltpu.VMEM((d, tn), BF16)],
        compiler_params=_params(("arbitrary", "arbitrary"), vmem), name="proj_" + mode,
    )(*args)
    return res if with_cache else res[0]


def _qk(q, k):
    return lax.dot_general(q, k, (((1,), (1,)), ((), ())), preferred_element_type=F32)


def _tdot(a, b):
    return lax.dot_general(a, b, (((0,), (0,)), ((), ())), preferred_element_type=F32)


def _attn_ctx_kernel(q_ref, k_ref, v_ref, o_ref, *, n_kv, n_g, hd):
    seq = q_ref.shape[0]

    def scores(kv):
        q = jnp.concatenate([q_ref[:, (kv * n_g + g) * hd:(kv * n_g + g + 1) * hd] for g in range(n_g)], axis=0)
        return _qk(k_ref[:, kv * hd:(kv + 1) * hd], q)

    ahead = [scores(kv) for kv in range(min(ATTN_LOOKAHEAD, n_kv))]
    for kv in range(n_kv):
        s = ahead.pop(0)
        p = jnp.exp2(s - jnp.max(s, axis=0, keepdims=True))
        den = jnp.sum(p, axis=0, keepdims=True)
        if kv + ATTN_LOOKAHEAD < n_kv:
            ahead.append(scores(kv + ATTN_LOOKAHEAD))
        o = (_tdot(v_ref[:, kv * hd:(kv + 1) * hd], p.astype(BF16)) / den).T
        for g in range(n_g):
            c0 = (kv * n_g + g) * hd
            o_ref[:, c0:c0 + hd] = o[g * seq:(g + 1) * seq, :].astype(BF16)


def _attn_lat_kernel(q_ref, k_ref, v_ref, ck_ref, cv_ref, o_ref, *, n_g, hd):
    k, v = k_ref[...], v_ref[...]
    ck, cv = ck_ref[...].astype(BF16), cv_ref[...].astype(BF16)
    tdot = lambda a, b: lax.dot_general(a, b, (((0,), (0,)), ((), ())), preferred_element_type=F32)

    def scores(g):
        q = q_ref[:, g * hd:(g + 1) * hd]
        return _qk(k, q), _qk(ck, q)

    ahead = [scores(g) for g in range(min(ATTN_LOOKAHEAD, n_g))]
    for g in range(n_g):
        s1, s2 = ahead.pop(0)
        mx = jnp.maximum(jnp.max(s1, axis=0, keepdims=True), jnp.max(s2, axis=0, keepdims=True))
        p1, p2 = jnp.exp2(s1 - mx), jnp.exp2(s2 - mx)
        den = jnp.sum(p1, axis=0, keepdims=True) + jnp.sum(p2, axis=0, keepdims=True)
        if g + ATTN_LOOKAHEAD < n_g:
            ahead.append(scores(g + ATTN_LOOKAHEAD))
        o = (tdot(v, p1.astype(BF16)) + tdot(cv, p2.astype(BF16))) / den
        o_ref[:, g * hd:(g + 1) * hd] = o.T.astype(BF16)


def _attention(q, k, v, cache_k, cache_v, layer, dims):
    m, qw = q.shape
    n_ctx, seq, dec_seq, kvw, hd = (dims[key] for key in ("n_ctx", "seq", "dec_seq", "kvw", "hd"))
    n_kv = kvw // hd
    n_g = qw // kvw
    dec_batch, _, past = cache_k.shape[:3]
    ck = cache_k.reshape(dec_batch, cache_k.shape[1], past, kvw)
    cv = cache_v.reshape(dec_batch, cache_v.shape[1], past, kvw)

    ctx = pl.pallas_call(
        functools.partial(_attn_ctx_kernel, n_kv=n_kv, n_g=n_g, hd=hd),
        grid=(n_ctx // seq,),
        in_specs=[pl.BlockSpec((seq, qw), lambda b: (b, 0)),
                  pl.BlockSpec((seq, kvw), lambda b: (b, 0)),
                  pl.BlockSpec((seq, kvw), lambda b: (b, 0))],
        out_specs=pl.BlockSpec((seq, qw), lambda b: (b, 0)),
        out_shape=jax.ShapeDtypeStruct((n_ctx, qw), BF16),
        compiler_params=_params(("arbitrary",), 4 * seq * (qw + kvw) * 2 + 8 * seq * seq * 4),
        name="attn_ctx",
    )(q, k, v)

    assert n_ctx % dec_seq == 0
    tq = _tile(dec_seq, ATTN_ROWS, SUBLANES)
    gw = n_g * hd
    qrow = lambda b, kv, t: ((n_ctx + b * dec_seq) // tq + t, kv)
    krow = lambda b, kv, t: (n_ctx // dec_seq + b, kv)
    crow = lambda b, kv, t: (b, layer, 0, kv)
    lat = pl.pallas_call(
        functools.partial(_attn_lat_kernel, n_g=n_g, hd=hd),
        grid=(dec_batch, n_kv, dec_seq // tq),
        in_specs=[pl.BlockSpec((tq, gw), qrow),
                  pl.BlockSpec((dec_seq, hd), krow), pl.BlockSpec((dec_seq, hd), krow),
                  pl.BlockSpec((None, None, past, hd), crow), pl.BlockSpec((None, None, past, hd), crow)],
        out_specs=pl.BlockSpec((tq, gw), lambda b, kv, t: (b * (dec_seq // tq) + t, kv)),
        out_shape=jax.ShapeDtypeStruct((m - n_ctx, qw), BF16),
        compiler_params=_params(("arbitrary",) * 3,
                                4 * tq * gw * 2 + 4 * dec_seq * hd * 2 + 4 * past * hd * 4
                                + 6 * tq * (dec_seq + past) * 4),
        name="attn_lat",
    )(q, k, v, ck, cv)
    return ctx, lat


def _pool_kernel(u_ref, band_ref, w_ref, sc_ref, o_ref, *, nct, seq, dec_seq, windows):
    i, g = pl.program_id(0), pl.program_id(1)
    rows = u_ref.shape[0]
    pb = band_ref.shape[-1]
    w = w_ref[...].astype(BF16)
    half = jnp.zeros((), jnp.int32)
    for gi, win in enumerate(windows):
        half = jnp.where(g == gi, win // 2, half)

    def window_sum(b, length):
        p0 = (b * pb) % length
        tot = jnp.dot(band_ref[1], u_ref[b * pb:(b + 1) * pb, :], preferred_element_type=F32)
        if p0 > 0:
            tot += jnp.dot(band_ref[0], u_ref[(b - 1) * pb:b * pb, :], preferred_element_type=F32)
        if p0 + pb < length:
            tot += jnp.dot(band_ref[2], u_ref[(b + 1) * pb:(b + 2) * pb, :], preferred_element_type=F32)
        return tot

    def run(length):
        nb = rows // pb
        tots = [window_sum(0, length)]
        for b in range(nb):
            if b + 1 < nb:
                tots.append(window_sum(b + 1, length))
            p0 = (b * pb) % length
            pos = p0 + lax.broadcasted_iota(jnp.int32, (pb, 1), 0)
            cnt = jnp.minimum(pos + half, length) - jnp.maximum(pos - half, 0)
            pooled = tots[b] / cnt.astype(F32) - u_ref[b * pb:(b + 1) * pb, :].astype(F32)
            mixed = jnp.dot(pooled.astype(BF16), w, preferred_element_type=F32)
            o_ref[b * pb:(b + 1) * pb, :] = (mixed * sc_ref[...]).astype(BF16)

    @pl.when(i < nct)
    def _():
        run(seq)

    @pl.when(i >= nct)
    def _():
        run(dec_seq)


def _pool_bands(pb):
    t = jnp.arange(pb)[:, None]
    j = jnp.arange(pb)[None, :]
    out = []
    for win in POOL_WINDOWS:
        h = win // 2
        prev = (j - pb - t) >= -h
        cur = ((j - t) >= -h) & ((j - t) <= h - 1)
        nxt = (j + pb - t) <= h - 1
        out.append(jnp.stack([prev, cur, nxt]))
    return jnp.stack(out).astype(BF16)


def _pool(u, bands, w_pool, pool_scale, layer, dims):
    m, pw = u.shape
    n_ctx, seq, dec_seq = dims["n_ctx"], dims["seq"], dims["dec_seq"]
    ng = len(POOL_WINDOWS)
    gd = pw // ng
    pb = bands.shape[-1]
    tr = dec_seq
    assert n_ctx % tr == 0 and tr % seq == 0 and seq % pb == 0 and max(POOL_WINDOWS) // 2 <= pb
    return pl.pallas_call(
        functools.partial(_pool_kernel, nct=n_ctx // tr, seq=seq, dec_seq=dec_seq, windows=POOL_WINDOWS),
        grid=(m // tr, ng),
        in_specs=[pl.BlockSpec((tr, gd), lambda i, g: (i, g)),
                  pl.BlockSpec((None, 3, pb, pb), lambda i, g: (g, 0, 0, 0)),
                  pl.BlockSpec((None, None, gd, gd), lambda i, g: (layer, g, 0, 0)),
                  pl.BlockSpec((1, gd), lambda i, g: (0, g))],
        out_specs=pl.BlockSpec((tr, gd), lambda i, g: (i, g)),
        out_shape=jax.ShapeDtypeStruct((m, pw), BF16),
        compiler_params=_params(("arbitrary", "arbitrary"),
                                4 * tr * gd * 2 + 2 * gd * gd * 4 + 6 * pb * pb * 2 + 8 * pb * gd * 4),
        name="pool_mix",
    )(u, bands, w_pool, pool_scale.reshape(1, pw))


def _outproj_kernel(*refs, n_exp, nct):
    (ac_ref, al_ref, p_ref, ga_ref, gb_ref, x_ref, w_ref, g1_ref, n2_ref, sc_ref, sh_ref) = refs[:11]
    rest = refs[11:]
    rows, d = x_ref.shape
    rsub = _tile(rows, OUTPROJ_CHUNK, SUBLANES) if n_exp else rows
    is_ctx = pl.program_id(0) < nct
    w = w_ref[...]
    xs, h2s = [], []
    for r0 in range(0, rows, rsub):
        sl = slice(r0, r0 + rsub)
        attn = jnp.where(is_ctx, ac_ref[sl, :], al_ref[sl, :])
        merged = (ga_ref[sl, :].astype(F32) * attn.astype(F32)
                  + gb_ref[sl, :].astype(F32) * p_ref[sl, :].astype(F32)).astype(BF16)
        x = x_ref[sl, :] + g1_ref[...] * jnp.dot(merged, w, preferred_element_type=F32)
        xs.append(x)
        h2s.append(_rms(x) * n2_ref[...] * (1.0 + sc_ref[...]) + sh_ref[...])
    if n_exp == 0:
        xo_ref, h_ref = rest
        for c, r0 in enumerate(range(0, rows, rsub)):
            xo_ref[r0:r0 + rsub, :] = xs[c]
            h_ref[r0:r0 + rsub, :] = h2s[c].astype(BF16)
        return
    rw_ref, xo_ref, h_ref, rt_ref = rest
    for c, r0 in enumerate(range(0, rows, rsub)):
        xo_ref[r0:r0 + rsub, :] = xs[c]
    h2 = jnp.concatenate(h2s, axis=0) if len(h2s) > 1 else h2s[0]
    chunks = d // LANES
    for s in range(chunks):
        h_ref[pl.ds(s, rows, stride=chunks), :] = h2[:, s * LANES:(s + 1) * LANES]
    h_hi = h2.astype(BF16)
    h_lo = (h2 - h_hi.astype(F32)).astype(BF16)
    part = (jnp.dot(h_hi, rw_ref[...], preferred_element_type=F32)
            + jnp.dot(h_lo, rw_ref[...], preferred_element_type=F32))
    logits = part[:, :LANES] + part[:, LANES:]
    lane = lax.broadcasted_iota(jnp.int32, logits.shape, 1)
    lane_f = lane.astype(F32)
    neg = jnp.float32(-jnp.inf)
    lg = jnp.where(lane < n_exp, logits, neg)
    m1 = jnp.max(lg, axis=-1, keepdims=True)
    i1 = jnp.min(jnp.where(lg == m1, lane_f, float(LANES)), axis=-1, keepdims=True)
    lg2 = jnp.where(lane_f == i1, neg, lg)
    m2 = jnp.max(lg2, axis=-1, keepdims=True)
    i2 = jnp.min(jnp.where(lg2 == m2, lane_f, float(LANES)), axis=-1, keepdims=True)
    e2 = jnp.exp(m2 - m1)
    den = 1.0 + e2
    rt = jnp.where(lane == 0, i1, 0.0)
    rt = jnp.where(lane == 1, i2, rt)
    rt = jnp.where(lane == 2, 1.0 / den, rt)
    rt_ref[...] = jnp.where(lane == 3, e2 / den, rt)


def _outproj(attn_ctx, attn_lat, pool, gates, x, w_out_b, g1, n2, sc2, sh2, grp, router_w):
    m, d = x.shape
    tm = _tile(math.gcd(m, attn_ctx.shape[0]), 256, SUBLANES)
    nct = attn_ctx.shape[0] // tm
    chunks = d // LANES
    row = pl.BlockSpec((tm, d), lambda i: (i, 0))
    vec = pl.BlockSpec((1, d), lambda i: (0, 0))
    args = [attn_ctx, attn_lat, pool, gates, gates, x, w_out_b, g1, n2.reshape(1, d), sc2, sh2]
    specs = [pl.BlockSpec((tm, d), lambda i: (jnp.minimum(i, nct - 1), 0)),
             pl.BlockSpec((tm, d), lambda i: (jnp.maximum(i - nct, 0), 0)),
             row, row, pl.BlockSpec((tm, d), lambda i: (i, 1)), row,
             pl.BlockSpec((d, d), lambda i: (0, 0)), _group_spec(d, grp), vec,
             _group_spec(d, grp), _group_spec(d, grp)]
    out_shape = [jax.ShapeDtypeStruct((m, d), F32)]
    out_specs = [row]
    n_exp = 0
    if router_w is None:
        out_shape.append(jax.ShapeDtypeStruct((m, d), BF16))
        out_specs.append(row)
    else:
        n_exp = router_w.shape[1]
        assert TOP_K == 2 and 2 * TOP_K <= LANES and n_exp <= LANES
        rw = jnp.pad(router_w, ((0, 0), (0, LANES - n_exp)))
        rw_hi = rw.astype(BF16)
        rw_lo = (rw - rw_hi.astype(F32)).astype(BF16)
        args.append(jnp.concatenate([rw_hi, rw_lo], axis=1))
        specs.append(pl.BlockSpec((d, 2 * LANES), lambda i: (0, 0)))
        out_shape += [jax.ShapeDtypeStruct((m * chunks, LANES), F32), jax.ShapeDtypeStruct((m, LANES), F32)]
        out_specs += [pl.BlockSpec((tm * chunks, LANES), lambda i: (i, 0)),
                      pl.BlockSpec((tm, LANES), lambda i: (i, 0))]
    return pl.pallas_call(
        functools.partial(_outproj_kernel, n_exp=n_exp, nct=nct),
        grid=(m // tm,), in_specs=specs, out_specs=out_specs, out_shape=out_shape,
        compiler_params=_params(("arbitrary",), 2 * tm * d * (4 * 2 + 3 * 4) + 2 * d * d * 2 + 8 * tm * d * 4),
        name="outproj_route" if n_exp else "outproj",
    )(*args)


def _changed(te_ref, i):
    return (i == 0) | (te_ref[i] != te_ref[jnp.maximum(i - 1, 0)])


def _up_kernel(te_ref, nu_ref, x_ref, wg_ref, wu_ref, o_ref, wgb, wub):
    i = pl.program_id(1)

    @pl.when(_changed(te_ref, i))
    def _():
        wgb[...] = wg_ref[...].astype(BF16)
        wub[...] = wu_ref[...].astype(BF16)

    @pl.when(i < nu_ref[0])
    def _():
        x = x_ref[...]
        a = jnp.dot(x, wgb[...], preferred_element_type=F32)
        b = jnp.dot(x, wub[...], preferred_element_type=F32)
        o_ref[...] = (a * _sigmoid(a) * b).astype(BF16)

    @pl.when(i >= nu_ref[0])
    def _():
        o_ref[...] = jnp.zeros(o_ref.shape, BF16)


def _down_kernel(te_ref, nu_ref, a_ref, w_ref, o_ref, wb):
    i = pl.program_id(1)

    @pl.when(_changed(te_ref, i))
    def _():
        wb[...] = w_ref[...].astype(BF16)

    o_ref[...] = jnp.dot(a_ref[...], wb[...], preferred_element_type=F32)


def _down_tok_kernel(te_ref, nu_ref, a_ref, w_ref, o_ref, *, chunks):
    i, j = pl.program_id(0), pl.program_id(1)
    tm = a_ref.shape[0]
    per = w_ref.shape[1] // LANES

    @pl.when(i < nu_ref[0])
    def _():
        acc = jnp.dot(a_ref[...], w_ref[...], preferred_element_type=F32)
        for jj in range(chunks // per):
            @pl.when(j == jj)
            def _():
                for c in range(per):
                    o_ref[pl.ds(jj * per + c, tm, stride=chunks), :] = acc[:, c * LANES:(c + 1) * LANES]

    @pl.when((i >= nu_ref[0]) & (j == 0))
    def _():
        o_ref[...] = jnp.zeros(o_ref.shape, F32)


def _swiglu(xs, tile_expert, n_used, tm, w_gate, w_up, w_down, sl, *, token_major):
    r, d = xs.shape
    f = w_gate.shape[3]
    tf = _tile(f, 512, LANES)
    tn = _tile(d, 512, LANES)
    nt = r // tm
    used = lambda i, nu: jnp.minimum(i, nu[0] - 1)
    act = pl.pallas_call(
        _up_kernel,
        grid_spec=pltpu.PrefetchScalarGridSpec(
            num_scalar_prefetch=2, grid=(f // tf, nt),
            in_specs=[pl.BlockSpec((tm, d), lambda j, i, te, nu: (used(i, nu), 0)),
                      pl.BlockSpec((None, None, d, tf), lambda j, i, te, nu: (sl, te[i], 0, j)),
                      pl.BlockSpec((None, None, d, tf), lambda j, i, te, nu: (sl, te[i], 0, j))],
            out_specs=pl.BlockSpec((tm, tf), lambda j, i, te, nu: (i, j)),
            scratch_shapes=[pltpu.VMEM((d, tf), BF16), pltpu.VMEM((d, tf), BF16)]),
        out_shape=jax.ShapeDtypeStruct((r, f), BF16),
        compiler_params=_params(("arbitrary", "arbitrary"),
                                2 * (tm * d * 2 + 2 * d * tf * 4 + tm * tf * 2) + 2 * d * tf * 2 + 4 * tm * tf * 4),
        name="swiglu_up",
    )(tile_expert, n_used, xs, w_gate, w_up)
    if not token_major:
        return pl.pallas_call(
            _down_kernel,
            grid_spec=pltpu.PrefetchScalarGridSpec(
                num_scalar_prefetch=2, grid=(d // tn, nt),
                in_specs=[pl.BlockSpec((tm, f), lambda j, i, te, nu: (i, 0)),
                          pl.BlockSpec((None, None, f, tn), lambda j, i, te, nu: (sl, te[i], 0, j))],
                out_specs=pl.BlockSpec((tm, tn), lambda j, i, te, nu: (i, j)),
                scratch_shapes=[pltpu.VMEM((f, tn), BF16)]),
            out_shape=jax.ShapeDtypeStruct((r, d), F32),
            compiler_params=_params(("arbitrary", "arbitrary"),
                                    2 * (tm * f * 2 + f * tn * 4 + tm * tn * 4) + f * tn * 2 + 2 * tm * tn * 4),
            name="swiglu_down",
        )(tile_expert, n_used, act, w_down)
    chunks = d // LANES
    tn = _tile(d, 1024, LANES)
    n_j = d // tn
    col = lambda i, j, nu: jnp.where(i < nu[0], j, n_j - 1)
    return pl.pallas_call(
        functools.partial(_down_tok_kernel, chunks=chunks),
        grid_spec=pltpu.PrefetchScalarGridSpec(
            num_scalar_prefetch=2, grid=(nt, n_j),
            in_specs=[pl.BlockSpec((tm, f), lambda i, j, te, nu: (used(i, nu), 0)),
                      pl.BlockSpec((None, None, f, tn), lambda i, j, te, nu: (sl, te[i], 0, col(i, j, nu)))],
            out_specs=pl.BlockSpec((tm * chunks, LANES), lambda i, j, te, nu: (i, 0))),
        out_shape=jax.ShapeDtypeStruct((r * chunks, LANES), F32),
        compiler_params=_params(("arbitrary", "arbitrary"),
                                2 * (tm * f * 2 + f * tn * 2 + tm * d * 4) + 4 * tm * tn * 4),
        name="swiglu_down_tok",
    )(tile_expert, n_used, act, w_down)


def _row_copy(src_hbm, src_row, dst, dst_row, sem, chunks):
    return pltpu.make_async_copy(
        src_hbm.at[pl.ds(pl.multiple_of(src_row * chunks, chunks), chunks), :],
        dst.at[pl.ds(pl.multiple_of(dst_row * chunks, chunks), chunks), :], sem)


def _start_rows(idx_ref, base, stride, n, src_hbm, dst, sem, chunks):
    def body(o, c):
        for u in range(DMA_UNROLL):
            r = o * DMA_UNROLL + u
            _row_copy(src_hbm, idx_ref[base + stride * r], dst, r, sem, chunks).start(priority=u % 2)
        return c
    lax.fori_loop(0, n // DMA_UNROLL, body, 0)


def _wait_rows(n, src_hbm, dst, sem, chunks):
    def body(o, c):
        for u in range(DMA_UNROLL):
            _row_copy(src_hbm, 0, dst, o * DMA_UNROLL + u, sem, chunks).wait()
        return c
    lax.fori_loop(0, n // DMA_UNROLL, body, 0)


def _gather_kernel(rt_ref, h_hbm, o_ref, buf, sems, *, chunks):
    i, n = pl.program_id(0), pl.num_programs(0)
    tg = o_ref.shape[0]
    slot = i % 2

    @pl.when(i == 0)
    def _():
        _start_rows(rt_ref, 0, 1, tg, h_hbm, buf.at[0], sems.at[0], chunks)

    @pl.when(i + 1 < n)
    def _():
        _start_rows(rt_ref, (i + 1) * tg, 1, tg, h_hbm, buf.at[1 - slot], sems.at[1 - slot], chunks)

    cur = buf.at[slot]
    _wait_rows(tg, h_hbm, cur, sems.at[slot], chunks)
    for s in range(chunks):
        o_ref[:, s * LANES:(s + 1) * LANES] = cur[pl.ds(s, tg, stride=chunks), :].astype(BF16)


def _gather_rows(h_tok, row_token, d):
    chunks = d // LANES
    r = row_token.shape[0]
    tg = _tile(r, 256, SUBLANES * DMA_UNROLL)
    return pl.pallas_call(
        functools.partial(_gather_kernel, chunks=chunks),
        grid_spec=pltpu.PrefetchScalarGridSpec(
            num_scalar_prefetch=1, grid=(r // tg,),
            in_specs=[pl.BlockSpec(memory_space=pl.ANY)],
            out_specs=pl.BlockSpec((tg, d), lambda i, rt: (i, 0)),
            scratch_shapes=[pltpu.VMEM((2, tg * chunks, LANES), F32), pltpu.SemaphoreType.DMA((2,))]),
        out_shape=jax.ShapeDtypeStruct((r, d), BF16),
        compiler_params=_params(("arbitrary",), 2 * tg * d * 4 + 2 * tg * d * 2 + 2 * tg * d * 4),
        name="moe_gather",
    )(row_token, h_tok)


def _combine_kernel(*refs, chunks, final, nct):
    pos_ref, o_hbm, rt_ref, x_ref, g_ref, n_ref = refs[:6]
    rest = list(refs[6:])
    if not final:
        sc_ref, sh_ref = rest[:2]
        rest = rest[2:]
    out_a, out_b, bufs, sems = rest
    i, n = pl.program_id(0), pl.num_programs(0)
    tc = x_ref.shape[0]
    slot = i % 2

    def start(step, sl):
        for k in range(TOP_K):
            _start_rows(pos_ref, TOP_K * step * tc + k, TOP_K, tc, o_hbm, bufs.at[sl, k], sems.at[sl, k], chunks)

    @pl.when(i == 0)
    def _():
        start(0, 0)

    @pl.when(i + 1 < n)
    def _():
        start(i + 1, 1 - slot)

    for k in range(TOP_K):
        _wait_rows(tc, o_hbm, bufs.at[slot, k], sems.at[slot, k], chunks)
    w0 = rt_ref[:, TOP_K:TOP_K + 1]
    w1 = rt_ref[:, TOP_K + 1:TOP_K + 2]
    b0, b1 = bufs.at[slot, 0], bufs.at[slot, 1]
    cols = [w0 * b0[pl.ds(s, tc, stride=chunks), :] + w1 * b1[pl.ds(s, tc, stride=chunks), :]
            for s in range(chunks)]
    x = x_ref[...] + g_ref[...] * jnp.concatenate(cols, axis=1)
    y = _rms(x) * n_ref[...]
    if final:
        @pl.when(i < nct)
        def _():
            out_a[...] = y

        @pl.when(i >= nct)
        def _():
            out_b[...] = y
    else:
        out_a[...] = x
        out_b[...] = (y * (1.0 + sc_ref[...]) + sh_ref[...]).astype(BF16)


def _combine(o_tok, pos, route, x, g, n, sc, sh, grp, n_ctx, *, final):
    m, d = x.shape
    chunks = d // LANES
    tc = _tile(math.gcd(m, n_ctx), 256, SUBLANES * DMA_UNROLL)
    nct = n_ctx // tc
    row = pl.BlockSpec((tc, d), lambda i, p: (i, 0))
    gspec = pl.BlockSpec((None, 1, d), lambda i, p: (grp(i), 0, 0))
    args = [pos, o_tok, route, x, g, n.reshape(1, d)]
    specs = [pl.BlockSpec(memory_space=pl.ANY), pl.BlockSpec((tc, LANES), lambda i, p: (i, 0)), row, gspec,
             pl.BlockSpec((1, d), lambda i, p: (0, 0))]
    if not final:
        args += [sc, sh]
        specs += [gspec, gspec]
    if final:
        out_specs = [pl.BlockSpec((tc, d), lambda i, p: (jnp.minimum(i, nct - 1), 0)),
                     pl.BlockSpec((tc, d), lambda i, p: (jnp.maximum(i - nct, 0), 0))]
        out_shape = [jax.ShapeDtypeStruct((n_ctx, d), F32), jax.ShapeDtypeStruct((m - n_ctx, d), F32)]
    else:
        out_specs = [row, row]
        out_shape = [jax.ShapeDtypeStruct((m, d), F32), jax.ShapeDtypeStruct((m, d), BF16)]
    return pl.pallas_call(
        functools.partial(_combine_kernel, chunks=chunks, final=final, nct=nct),
        grid_spec=pltpu.PrefetchScalarGridSpec(
            num_scalar_prefetch=1, grid=(m // tc,), in_specs=specs,
            out_specs=out_specs,
            scratch_shapes=[pltpu.VMEM((2, TOP_K, tc * chunks, LANES), F32),
                            pltpu.SemaphoreType.DMA((2, TOP_K))]),
        out_shape=out_shape,
        compiler_params=_params(("arbitrary",), 2 * TOP_K * tc * d * 4 + 6 * tc * d * 4 + 6 * tc * d * 4),
        name="moe_combine",
    )(*args)


def _route_plan(route, n_exp, tm):
    m = route.shape[0]
    flat_e = route[:, :TOP_K].astype(jnp.int32).reshape(-1)
    onehot = (flat_e[:, None] == jnp.arange(n_exp)[None, :]).astype(jnp.int32)
    csum = jnp.cumsum(onehot, axis=0)
    rank = jnp.take_along_axis(csum, flat_e[:, None], axis=1)[:, 0] - 1
    counts = csum[-1]
    padded = (counts + tm - 1) // tm * tm
    starts = jnp.cumsum(padded) - padded
    pos = starts[flat_e] + rank
    n_rows = TOP_K * m + n_exp * tm
    row_token = jnp.zeros((n_rows,), jnp.int32).at[pos].set(jnp.arange(TOP_K * m, dtype=jnp.int32) // TOP_K)
    tiles = jnp.arange(n_rows // tm, dtype=jnp.int32)
    tile_expert = jnp.sum(tiles[:, None] * tm >= starts[None, :], axis=1).astype(jnp.int32) - 1
    n_used = (jnp.sum(padded) // tm).astype(jnp.int32).reshape(1)
    return pos.astype(jnp.int32), row_token, tile_expert, n_used


def _rope_tables(dec_seq, hd):
    half = hd // 4
    rows = dec_seq // GRID_W
    row_pos = jnp.broadcast_to(jnp.arange(rows)[:, None], (rows, GRID_W)).reshape(-1).astype(F32)
    col_pos = jnp.broadcast_to(jnp.arange(GRID_W)[None, :], (rows, GRID_W)).reshape(-1).astype(F32)
    inv_freq = ROPE_THETA ** (-jnp.arange(0, 2 * half, 2, dtype=F32) / (2 * half))
    ang_r = row_pos[:, None] * inv_freq[None, :]
    ang_c = col_pos[:, None] * inv_freq[None, :]
    cos = jnp.concatenate([jnp.cos(ang_r)] * 2 + [jnp.cos(ang_c)] * 2, axis=1)
    sin = jnp.concatenate([-jnp.sin(ang_r), jnp.sin(ang_r), -jnp.sin(ang_c), jnp.sin(ang_c)], axis=1)
    return cos, sin


def kernel(x_prompt, x_sample, c, cache_k, cache_v, c_ctx, w_mod, b_mod, norm1_g, norm2_g, w_in,
           q_norm_g, k_norm_g, w_pool, pool_scale, w_out, dense_w_gate, dense_w_up, dense_w_down,
           router_w, moe_w_gate, moe_w_up, moe_w_down, final_norm_g):
    batch, seq, d = x_prompt.shape
    dec_batch, dec_seq, _ = x_sample.shape
    depth = w_mod.shape[0]
    n_kv, hd = cache_k.shape[3], cache_k.shape[4]
    kvw = n_kv * hd
    qw = d
    pw = pool_scale.shape[1]
    n_ctx = batch * seq
    m = n_ctx + dec_batch * dec_seq
    dims = dict(n_ctx=n_ctx, seq=seq, dec_seq=dec_seq, qw=qw, kvw=kvw, pw=pw, hd=hd)
    n_exp = router_w.shape[2]
    gw = w_in.shape[2] - qw - 2 * kvw - pw
    assert hd == LANES and d % LANES == 0 and gw == 2 * d

    def grp_for(tile_rows):
        return lambda i: jnp.maximum((i * tile_rows - n_ctx) // dec_seq + 1, 0)

    n_grp = 1 + dec_batch
    rows = -(-n_grp // SUBLANES) * SUBLANES
    cs = jnp.zeros((rows, d), F32).at[0].set(c_ctx).at[1:n_grp].set(c)
    mod = _modulation(cs, w_mod, b_mod)[:, :n_grp]
    mod = mod.reshape(depth, n_grp, N_MOD, 1, d).transpose(0, 2, 1, 3, 4)

    tabs = _rope_tables(dec_seq, hd)
    bands = _pool_bands(min(POOL_BLOCK, seq))
    w_out_b = w_out.astype(BF16)
    moe_w_down_b = moe_w_down.astype(BF16)

    tm_rn = _tile(math.gcd(m, n_ctx), 512, SUBLANES)
    tm_op = _tile(math.gcd(m, n_ctx), 256, SUBLANES)
    assert tm_rn == _tile(m, 512, SUBLANES) and dec_seq % tm_rn == 0 and dec_seq % tm_op == 0
    x, h = _embed_norm(x_prompt.reshape(n_ctx, d), x_sample.reshape(-1, d), norm1_g[0], mod[0, 1], mod[0, 0],
                       grp_for(tm_rn))

    new_k, new_v = [], []
    for l in range(depth):
        sh1, sc1, g1, sh2, sc2, g2 = (mod[l, k] for k in range(N_MOD))
        q = _proj(h, w_in, l, 0, qw, dims, mode="q", gain=q_norm_g[l], tabs=tabs)
        k, kc = _proj(h, w_in, l, qw, kvw, dims, mode="k", gain=k_norm_g[l], tabs=tabs, with_cache=True)
        v, vc = _proj(h, w_in, l, qw + kvw, kvw, dims, mode="v", with_cache=True)
        u = _proj(h, w_in, l, qw + 2 * kvw, pw, dims, mode="u")
        gates = _proj(h, w_in, l, qw + 2 * kvw + pw, gw, dims, mode="gate")
        new_k.append(kc.reshape(batch, seq, n_kv, hd))
        new_v.append(vc.reshape(batch, seq, n_kv, hd))
        attn_ctx, attn_lat = _attention(q, k, v, cache_k, cache_v, l, dims)
        pool = _pool(u, bands, w_pool, pool_scale[l:l + 1], l, dims)
        last = l == depth - 1
        if last:
            n_next, sc_next, sh_next = final_norm_g, None, None
        else:
            n_next, sc_next, sh_next = norm1_g[l + 1], mod[l + 1, 1], mod[l + 1, 0]
        j = l // 2
        if l % 2 == 0:
            x, h2 = _outproj(attn_ctx, attn_lat, pool, gates, x, w_out_b[l], g1, norm2_g[l], sc2, sh2, grp_for(tm_op), None)
            tm = _tile(m, DENSE_FFN_ROWS, SUBLANES)
            nt = jnp.full((1,), m // tm, jnp.int32)
            o = _swiglu(h2, jnp.zeros((m // tm,), jnp.int32), nt, tm,
                        dense_w_gate[:, None], dense_w_up[:, None], dense_w_down[:, None], j, token_major=False)
            x, h = _resid_norm(x, o, g2, n_next, sc_next, sh_next, grp_for(tm_rn), final=last)
        else:
            x, h2_tok, route = _outproj(attn_ctx, attn_lat, pool, gates, x, w_out_b[l], g1, norm2_g[l], sc2, sh2,
                                        grp_for(tm_op), router_w[j])
            tm = _tile(m, 512, SUBLANES)
            pos, row_token, tile_expert, n_used = _route_plan(route, n_exp, tm)
            xs = _gather_rows(h2_tok, row_token, d)
            o_tok = _swiglu(xs, tile_expert, n_used, tm, moe_w_gate, moe_w_up, moe_w_down_b, j, token_major=True)
            x, h = _combine(o_tok, pos, route, x, g2, n_next, sc_next, sh_next, grp_for(tm_op), n_ctx, final=last)
            if last:
                y_ctx, y_lat = x, h

    if depth % 2 == 1:
        y_ctx, y_lat = h[:n_ctx], h[n_ctx:]
    y_prompt = y_ctx.reshape(batch, seq, d)
    y_sample = y_lat.reshape(dec_batch, dec_seq, d)
    return (y_prompt, y_sample, jnp.stack(new_k, axis=1), jnp.stack(new_v, axis=1))
```

```python
import functools
import math

import jax
import jax.numpy as jnp
from jax import lax
from jax.experimental import pallas as pl
from jax.experimental.pallas import tpu as pltpu

GRID_W = 64
ROPE_THETA = 10000.0
POOL_WINDOWS = (2, 4, 8, 16)
TOP_K = 2
N_MOD = 6
EPS = 1e-6

LANES = 128
SUBLANES = 8
VMEM_BUDGET = 60000 * 1024
POOL_BLOCK = 256
ROW_CHUNK = 256
DMA_UNROLL = 8
QK_SCALE_LOG2E = math.log2(math.e)
ATTN_ROWS = 512
PROJ_ROWS = 2048
DENSE_FFN_ROWS = 1024
MOE_ROWS = 512
OUTPROJ_CHUNK = 128
ATTN_LOOKAHEAD = 2

F32 = jnp.float32
BF16 = jnp.bfloat16


def _params(semantics, vmem_bytes):
    limit = int(min(VMEM_BUDGET, vmem_bytes * 5 // 4 + (4 << 20)))
    return pltpu.CompilerParams(dimension_semantics=semantics, vmem_limit_bytes=limit)


def _tile(n, pref, align):
    t = min(n, pref)
    t -= t % align
    while t > align and n % t:
        t -= align
    assert t >= align and n % t == 0, (n, pref, align)
    return t


def _rms(x):
    return x * lax.rsqrt(jnp.mean(x * x, axis=-1, keepdims=True) + EPS)


def _sigmoid(x):
    return 1.0 / (1.0 + jnp.exp(-x))


def _mod_kernel(c_ref, w_ref, b_ref, o_ref):
    c = c_ref[...]
    s = c * _sigmoid(c)
    w = w_ref[...]
    s_hi, w_hi = s.astype(BF16), w.astype(BF16)
    s_lo = (s - s_hi.astype(F32)).astype(BF16)
    w_lo = (w - w_hi.astype(F32)).astype(BF16)
    dot = functools.partial(jnp.dot, preferred_element_type=F32)
    o_ref[...] = dot(s_hi, w_hi) + dot(s_lo, w_hi) + dot(s_hi, w_lo) + b_ref[...]


def _modulation(cs, w_mod, b_mod):
    depth, d, n = w_mod.shape
    rows = cs.shape[0]
    tn = _tile(n, 1024, LANES)
    return pl.pallas_call(
        _mod_kernel,
        grid=(depth, n // tn),
        in_specs=[pl.BlockSpec((rows, d), lambda l, j: (0, 0)),
                  pl.BlockSpec((None, d, tn), lambda l, j: (l, 0, j)),
                  pl.BlockSpec((None, 1, tn), lambda l, j: (l, 0, j))],
        out_specs=pl.BlockSpec((None, rows, tn), lambda l, j: (l, 0, j)),
        out_shape=jax.ShapeDtypeStruct((depth, rows, n), F32),
        compiler_params=_params(("arbitrary", "arbitrary"), 2 * d * tn * 4 + 4 * rows * (d + tn) * 4),
        name="modulation",
    )(cs, w_mod, b_mod.reshape(depth, 1, n))


def _resid_norm_kernel(x_ref, o_ref, g_ref, n_ref, *rest, final):
    x = x_ref[...] + g_ref[...] * o_ref[...]
    y = _rms(x) * n_ref[...]
    if final:
        (y_ref,) = rest
        y_ref[...] = y
    else:
        sc_ref, sh_ref, xo_ref, h_ref = rest
        xo_ref[...] = x
        h_ref[...] = (y * (1.0 + sc_ref[...]) + sh_ref[...]).astype(BF16)


def _group_spec(d, grp):
    return pl.BlockSpec((None, 1, d), lambda i: (grp(i), 0, 0))


def _resid_norm(x, o, g, n, sc, sh, grp, *, final):
    m, d = x.shape
    tm = _tile(m, 512, SUBLANES)
    row = pl.BlockSpec((tm, d), lambda i: (i, 0))
    args = [x, o, g, n.reshape(1, d)]
    specs = [row, row, _group_spec(d, grp), pl.BlockSpec((1, d), lambda i: (0, 0))]
    if final:
        out_shape, out_specs = [jax.ShapeDtypeStruct((m, d), F32)], [row]
    else:
        args += [sc, sh]
        specs += [_group_spec(d, grp)] * 2
        out_shape = [jax.ShapeDtypeStruct((m, d), F32), jax.ShapeDtypeStruct((m, d), BF16)]
        out_specs = [row, row]
    res = pl.pallas_call(
        functools.partial(_resid_norm_kernel, final=final),
        grid=(m // tm,), in_specs=specs, out_specs=out_specs, out_shape=out_shape,
        compiler_params=_params(("arbitrary",), 8 * tm * d * 4),
        name="resid_norm",
    )(*args)
    return (None, res[0]) if final else res


def _embed_norm_kernel(xp_ref, xs_ref, n_ref, sc_ref, sh_ref, x_ref, h_ref, *, nct):
    x = jnp.where(pl.program_id(0) < nct, xp_ref[...], xs_ref[...])
    x_ref[...] = x
    h_ref[...] = (_rms(x) * n_ref[...] * (1.0 + sc_ref[...]) + sh_ref[...]).astype(BF16)


def _embed_norm(xp, xs, n, sc, sh, grp):
    n_ctx, d = xp.shape
    m = n_ctx + xs.shape[0]
    tm = _tile(math.gcd(m, n_ctx), 512, SUBLANES)
    nct = n_ctx // tm
    row = pl.BlockSpec((tm, d), lambda i: (i, 0))
    return pl.pallas_call(
        functools.partial(_embed_norm_kernel, nct=nct),
        grid=(m // tm,),
        in_specs=[pl.BlockSpec((tm, d), lambda i: (jnp.minimum(i, nct - 1), 0)),
                  pl.BlockSpec((tm, d), lambda i: (jnp.maximum(i - nct, 0), 0)),
                  pl.BlockSpec((1, d), lambda i: (0, 0)), _group_spec(d, grp), _group_spec(d, grp)],
        out_specs=[row, row],
        out_shape=[jax.ShapeDtypeStruct((m, d), F32), jax.ShapeDtypeStruct((m, d), BF16)],
        compiler_params=_params(("arbitrary",), 10 * tm * d * 4),
        name="embed_norm",
    )(xp, xs, n.reshape(1, d), sc, sh)


def _rope(y, cos, sin):
    half = LANES // 4
    lane = lax.broadcasted_iota(jnp.int32, y.shape, 1)
    partner = jnp.where((lane & (2 * half - 1)) < half,
                        pltpu.roll(y, LANES - half, axis=1), pltpu.roll(y, half, axis=1))
    return y * cos + partner * sin


def _cast_weights_once(w_ref, wb):
    @pl.when(pl.program_id(1) == 0)
    def _():
        wb[...] = w_ref[...].astype(BF16)


def _proj_chunks(h_ref, wb, epilogue):
    w = wb[...]
    rows = h_ref.shape[0]
    rsub = _tile(rows, ROW_CHUNK, SUBLANES)
    for rc in range(rows // rsub):
        sl = slice(rc * rsub, (rc + 1) * rsub)
        epilogue(jnp.dot(h_ref[sl, :], w, preferred_element_type=F32), sl)


def _qk_proj_kernel(h_ref, w_ref, g_ref, cos_ref, sin_ref, o_ref, *rest, nct, hd, mult):
    cache, wb = rest[:-1], rest[-1]
    _cast_weights_once(w_ref, wb)

    def run(is_ctx):
        def epilogue(acc, sl):
            cols = []
            for hh in range(acc.shape[1] // hd):
                y = _rms(acc[:, hh * hd:(hh + 1) * hd]) * g_ref[...]
                if not is_ctx:
                    y = _rope(y, cos_ref[sl, :], sin_ref[sl, :])
                cols.append(y)
            y = jnp.concatenate(cols, axis=1) if len(cols) > 1 else cols[0]
            if is_ctx and cache:
                cache[0][sl, :] = y
            o_ref[sl, :] = (y * mult if mult != 1.0 else y).astype(BF16)
        _proj_chunks(h_ref, wb, epilogue)

    i = pl.program_id(1)
    pl.when(i < nct)(lambda: run(True))
    pl.when(i >= nct)(lambda: run(False))


def _plain_proj_kernel(h_ref, w_ref, o_ref, *rest, nct, gate):
    cache, wb = rest[:-1], rest[-1]
    _cast_weights_once(w_ref, wb)

    def run(to_cache):
        def epilogue(acc, sl):
            o_ref[sl, :] = (_sigmoid(acc) if gate else acc).astype(BF16)
            if to_cache:
                cache[0][sl, :] = acc
        _proj_chunks(h_ref, wb, epilogue)

    if cache:
        i = pl.program_id(1)
        pl.when(i < nct)(lambda: run(True))
        pl.when(i >= nct)(lambda: run(False))
    else:
        run(False)


def _proj(h, w_in, layer, col0, width, dims, *, mode, gain=None, tabs=None, with_cache=False):
    m, d = h.shape
    n_ctx, dec_seq, hd = dims["n_ctx"], dims["dec_seq"], dims["hd"]
    tn = _tile(math.gcd(width, col0), 512, LANES)
    tm = _tile(dec_seq, PROJ_ROWS, SUBLANES)
    assert n_ctx % tm == 0 and col0 % tn == 0 and width % tn == 0
    nct, tps = n_ctx // tm, dec_seq // tm
    args = [h, w_in]
    specs = [pl.BlockSpec((tm, d), lambda j, i: (i, 0)),
             pl.BlockSpec((None, d, tn), lambda j, i: (layer, 0, col0 // tn + j))]
    if mode in ("q", "k"):
        tab = pl.BlockSpec((tm, hd), lambda j, i: (jnp.maximum(i - nct, 0) % tps, 0))
        args += [gain.reshape(1, hd), tabs[0], tabs[1]]
        specs += [pl.BlockSpec((1, hd), lambda j, i: (0, 0)), tab, tab]
        body = functools.partial(_qk_proj_kernel, nct=nct, hd=hd,
                                 mult=QK_SCALE_LOG2E * float(hd) ** -0.5 if mode == "q" else 1.0)
    else:
        body = functools.partial(_plain_proj_kernel, nct=nct, gate=mode == "gate")
    out_shape = [jax.ShapeDtypeStruct((m, width), BF16)]
    out_specs = [pl.BlockSpec((tm, tn), lambda j, i: (i, j))]
    if with_cache:
        out_shape.append(jax.ShapeDtypeStruct((n_ctx, width), F32))
        out_specs.append(pl.BlockSpec((tm, tn), lambda j, i: (jnp.minimum(i, nct - 1), j)))
    vmem = (2 * (tm * d * 2 + d * tn * 4 + 2 * tm * hd * 4 + tm * tn * 2 + tm * tn * 4)
            + d * tn * 2 + 8 * ROW_CHUNK * tn * 4)
    res = pl.pallas_call(
        body, grid=(width // tn, m // tm), in_specs=specs, out_specs=out_specs, out_shape=out_shape,
        scratch_shapes=[pltpu.VMEM((d, tn), BF16)],
        compiler_params=_params(("arbitrary", "arbitrary"), vmem), name="proj_" + mode,
    )(*args)
    return res if with_cache else res[0]


def _qk(q, k):
    return lax.dot_general(q, k, (((1,), (1,)), ((), ())), preferred_element_type=F32)


def _attn_ctx_kernel(q_ref, k_ref, v_ref, o_ref, *, n_kv, n_g, hd):
    for kv in range(n_kv):
        k = k_ref[:, kv * hd:(kv + 1) * hd]
        v = v_ref[:, kv * hd:(kv + 1) * hd]
        for g in range(n_g):
            c0 = (kv * n_g + g) * hd
            s = _qk(q_ref[:, c0:c0 + hd], k)
            p = jnp.exp2(s - jnp.max(s, axis=-1, keepdims=True))
            o = jnp.dot(p.astype(BF16), v, preferred_element_type=F32)
            o_ref[:, c0:c0 + hd] = (o / jnp.sum(p, axis=-1, keepdims=True)).astype(BF16)


def _attn_lat_kernel(q_ref, k_ref, v_ref, ck_ref, cv_ref, o_ref, *, n_g, hd):
    k, v = k_ref[...], v_ref[...]
    ck, cv = ck_ref[...].astype(BF16), cv_ref[...].astype(BF16)
    tdot = lambda a, b: lax.dot_general(a, b, (((0,), (0,)), ((), ())), preferred_element_type=F32)

    def scores(g):
        q = q_ref[:, g * hd:(g + 1) * hd]
        return _qk(k, q), _qk(ck, q)

    ahead = [scores(g) for g in range(min(ATTN_LOOKAHEAD, n_g))]
    for g in range(n_g):
        s1, s2 = ahead.pop(0)
        mx = jnp.maximum(jnp.max(s1, axis=0, keepdims=True), jnp.max(s2, axis=0, keepdims=True))
        p1, p2 = jnp.exp2(s1 - mx), jnp.exp2(s2 - mx)
        den = jnp.sum(p1, axis=0, keepdims=True) + jnp.sum(p2, axis=0, keepdims=True)
        if g + ATTN_LOOKAHEAD < n_g:
            ahead.append(scores(g + ATTN_LOOKAHEAD))
        o = (tdot(v, p1.astype(BF16)) + tdot(cv, p2.astype(BF16))) / den
        o_ref[:, g * hd:(g + 1) * hd] = o.T.astype(BF16)


def _attention(q, k, v, cache_k, cache_v, layer, dims):
    m, qw = q.shape
    n_ctx, seq, dec_seq, kvw, hd = (dims[key] for key in ("n_ctx", "seq", "dec_seq", "kvw", "hd"))
    n_kv = kvw // hd
    n_g = qw // kvw
    dec_batch, _, past = cache_k.shape[:3]
    ck = cache_k.reshape(dec_batch, cache_k.shape[1], past, kvw)
    cv = cache_v.reshape(dec_batch, cache_v.shape[1], past, kvw)

    ctx = pl.pallas_call(
        functools.partial(_attn_ctx_kernel, n_kv=n_kv, n_g=n_g, hd=hd),
        grid=(n_ctx // seq,),
        in_specs=[pl.BlockSpec((seq, qw), lambda b: (b, 0)),
                  pl.BlockSpec((seq, kvw), lambda b: (b, 0)),
                  pl.BlockSpec((seq, kvw), lambda b: (b, 0))],
        out_specs=pl.BlockSpec((seq, qw), lambda b: (b, 0)),
        out_shape=jax.ShapeDtypeStruct((n_ctx, qw), BF16),
        compiler_params=_params(("arbitrary",), 4 * seq * (qw + kvw) * 2 + 8 * seq * seq * 4),
        name="attn_ctx",
    )(q, k, v)

    assert n_ctx % dec_seq == 0
    tq = _tile(dec_seq, ATTN_ROWS, SUBLANES)
    gw = n_g * hd
    qrow = lambda b, kv, t: ((n_ctx + b * dec_seq) // tq + t, kv)
    krow = lambda b, kv, t: (n_ctx // dec_seq + b, kv)
    crow = lambda b, kv, t: (b, layer, 0, kv)
    lat = pl.pallas_call(
        functools.partial(_attn_lat_kernel, n_g=n_g, hd=hd),
        grid=(dec_batch, n_kv, dec_seq // tq),
        in_specs=[pl.BlockSpec((tq, gw), qrow),
                  pl.BlockSpec((dec_seq, hd), krow), pl.BlockSpec((dec_seq, hd), krow),
                  pl.BlockSpec((None, None, past, hd), crow), pl.BlockSpec((None, None, past, hd), crow)],
        out_specs=pl.BlockSpec((tq, gw), lambda b, kv, t: (b * (dec_seq // tq) + t, kv)),
        out_shape=jax.ShapeDtypeStruct((m - n_ctx, qw), BF16),
        compiler_params=_params(("arbitrary",) * 3,
                                4 * tq * gw * 2 + 4 * dec_seq * hd * 2 + 4 * past * hd * 4
                                + 6 * tq * (dec_seq + past) * 4),
        name="attn_lat",
    )(q, k, v, ck, cv)
    return ctx, lat


def _pool_kernel(u_ref, band_ref, w_ref, sc_ref, o_ref, *, nct, seq, dec_seq, windows):
    i, g = pl.program_id(0), pl.program_id(1)
    rows = u_ref.shape[0]
    pb = band_ref.shape[-1]
    w = w_ref[...].astype(BF16)
    half = jnp.zeros((), jnp.int32)
    for gi, win in enumerate(windows):
        half = jnp.where(g == gi, win // 2, half)

    def window_sum(b, length):
        p0 = (b * pb) % length
        tot = jnp.dot(band_ref[1], u_ref[b * pb:(b + 1) * pb, :], preferred_element_type=F32)
        if p0 > 0:
            tot += jnp.dot(band_ref[0], u_ref[(b - 1) * pb:b * pb, :], preferred_element_type=F32)
        if p0 + pb < length:
            tot += jnp.dot(band_ref[2], u_ref[(b + 1) * pb:(b + 2) * pb, :], preferred_element_type=F32)
        return tot

    def run(length):
        nb = rows // pb
        tots = [window_sum(0, length)]
        for b in range(nb):
            if b + 1 < nb:
                tots.append(window_sum(b + 1, length))
            p0 = (b * pb) % length
            pos = p0 + lax.broadcasted_iota(jnp.int32, (pb, 1), 0)
            cnt = jnp.minimum(pos + half, length) - jnp.maximum(pos - half, 0)
            pooled = tots[b] / cnt.astype(F32) - u_ref[b * pb:(b + 1) * pb, :].astype(F32)
            mixed = jnp.dot(pooled.astype(BF16), w, preferred_element_type=F32)
            o_ref[b * pb:(b + 1) * pb, :] = (mixed * sc_ref[...]).astype(BF16)

    @pl.when(i < nct)
    def _():
        run(seq)

    @pl.when(i >= nct)
    def _():
        run(dec_seq)


def _pool_bands(pb):
    t = jnp.arange(pb)[:, None]
    j = jnp.arange(pb)[None, :]
    out = []
    for win in POOL_WINDOWS:
        h = win // 2
        prev = (j - pb - t) >= -h
        cur = ((j - t) >= -h) & ((j - t) <= h - 1)
        nxt = (j + pb - t) <= h - 1
        out.append(jnp.stack([prev, cur, nxt]))
    return jnp.stack(out).astype(BF16)


def _pool(u, bands, w_pool, pool_scale, layer, dims):
    m, pw = u.shape
    n_ctx, seq, dec_seq = dims["n_ctx"], dims["seq"], dims["dec_seq"]
    ng = len(POOL_WINDOWS)
    gd = pw // ng
    pb = bands.shape[-1]
    tr = dec_seq
    assert n_ctx % tr == 0 and tr % seq == 0 and seq % pb == 0 and max(POOL_WINDOWS) // 2 <= pb
    return pl.pallas_call(
        functools.partial(_pool_kernel, nct=n_ctx // tr, seq=seq, dec_seq=dec_seq, windows=POOL_WINDOWS),
        grid=(m // tr, ng),
        in_specs=[pl.BlockSpec((tr, gd), lambda i, g: (i, g)),
                  pl.BlockSpec((None, 3, pb, pb), lambda i, g: (g, 0, 0, 0)),
                  pl.BlockSpec((None, None, gd, gd), lambda i, g: (layer, g, 0, 0)),
                  pl.BlockSpec((1, gd), lambda i, g: (0, g))],
        out_specs=pl.BlockSpec((tr, gd), lambda i, g: (i, g)),
        out_shape=jax.ShapeDtypeStruct((m, pw), BF16),
        compiler_params=_params(("arbitrary", "arbitrary"),
                                4 * tr * gd * 2 + 2 * gd * gd * 4 + 6 * pb * pb * 2 + 8 * pb * gd * 4),
        name="pool_mix",
    )(u, bands, w_pool, pool_scale.reshape(1, pw))


def _outproj_kernel(*refs, n_exp, nct):
    (ac_ref, al_ref, p_ref, ga_ref, gb_ref, x_ref, w_ref, g1_ref, n2_ref, sc_ref, sh_ref) = refs[:11]
    rest = refs[11:]
    rows, d = x_ref.shape
    rsub = _tile(rows, OUTPROJ_CHUNK, SUBLANES) if n_exp else rows
    is_ctx = pl.program_id(0) < nct
    w = w_ref[...]
    xs, h2s = [], []
    for r0 in range(0, rows, rsub):
        sl = slice(r0, r0 + rsub)
        attn = jnp.where(is_ctx, ac_ref[sl, :], al_ref[sl, :])
        merged = (ga_ref[sl, :].astype(F32) * attn.astype(F32)
                  + gb_ref[sl, :].astype(F32) * p_ref[sl, :].astype(F32)).astype(BF16)
        x = x_ref[sl, :] + g1_ref[...] * jnp.dot(merged, w, preferred_element_type=F32)
        xs.append(x)
        h2s.append(_rms(x) * n2_ref[...] * (1.0 + sc_ref[...]) + sh_ref[...])
    if n_exp == 0:
        xo_ref, h_ref = rest
        for c, r0 in enumerate(range(0, rows, rsub)):
            xo_ref[r0:r0 + rsub, :] = xs[c]
            h_ref[r0:r0 + rsub, :] = h2s[c].astype(BF16)
        return
    rw_ref, xo_ref, h_ref, rt_ref = rest
    for c, r0 in enumerate(range(0, rows, rsub)):
        xo_ref[r0:r0 + rsub, :] = xs[c]
    h2 = jnp.concatenate(h2s, axis=0) if len(h2s) > 1 else h2s[0]
    chunks = d // LANES
    for s in range(chunks):
        h_ref[pl.ds(s, rows, stride=chunks), :] = h2[:, s * LANES:(s + 1) * LANES]
    h_hi = h2.astype(BF16)
    h_lo = (h2 - h_hi.astype(F32)).astype(BF16)
    part = (jnp.dot(h_hi, rw_ref[...], preferred_element_type=F32)
            + jnp.dot(h_lo, rw_ref[...], preferred_element_type=F32))
    logits = part[:, :LANES] + part[:, LANES:]
    lane = lax.broadcasted_iota(jnp.int32, logits.shape, 1)
    lane_f = lane.astype(F32)
    neg = jnp.float32(-jnp.inf)
    lg = jnp.where(lane < n_exp, logits, neg)
    m1 = jnp.max(lg, axis=-1, keepdims=True)
    i1 = jnp.min(jnp.where(lg == m1, lane_f, float(LANES)), axis=-1, keepdims=True)
    lg2 = jnp.where(lane_f == i1, neg, lg)
    m2 = jnp.max(lg2, axis=-1, keepdims=True)
    i2 = jnp.min(jnp.where(lg2 == m2, lane_f, float(LANES)), axis=-1, keepdims=True)
    e2 = jnp.exp(m2 - m1)
    den = 1.0 + e2
    rt = jnp.where(lane == 0, i1, 0.0)
    rt = jnp.where(lane == 1, i2, rt)
    rt = jnp.where(lane == 2, 1.0 / den, rt)
    rt_ref[...] = jnp.where(lane == 3, e2 / den, rt)


def _outproj(attn_ctx, attn_lat, pool, gates, x, w_out_b, g1, n2, sc2, sh2, grp, router_w):
    m, d = x.shape
    tm = _tile(math.gcd(m, attn_ctx.shape[0]), 256, SUBLANES)
    nct = attn_ctx.shape[0] // tm
    chunks = d // LANES
    row = pl.BlockSpec((tm, d), lambda i: (i, 0))
    vec = pl.BlockSpec((1, d), lambda i: (0, 0))
    args = [attn_ctx, attn_lat, pool, gates, gates, x, w_out_b, g1, n2.reshape(1, d), sc2, sh2]
    specs = [pl.BlockSpec((tm, d), lambda i: (jnp.minimum(i, nct - 1), 0)),
             pl.BlockSpec((tm, d), lambda i: (jnp.maximum(i - nct, 0), 0)),
             row, row, pl.BlockSpec((tm, d), lambda i: (i, 1)), row,
             pl.BlockSpec((d, d), lambda i: (0, 0)), _group_spec(d, grp), vec,
             _group_spec(d, grp), _group_spec(d, grp)]
    out_shape = [jax.ShapeDtypeStruct((m, d), F32)]
    out_specs = [row]
    n_exp = 0
    if router_w is None:
        out_shape.append(jax.ShapeDtypeStruct((m, d), BF16))
        out_specs.append(row)
    else:
        n_exp = router_w.shape[1]
        assert TOP_K == 2 and 2 * TOP_K <= LANES and n_exp <= LANES
        rw = jnp.pad(router_w, ((0, 0), (0, LANES - n_exp)))
        rw_hi = rw.astype(BF16)
        rw_lo = (rw - rw_hi.astype(F32)).astype(BF16)
        args.append(jnp.concatenate([rw_hi, rw_lo], axis=1))
        specs.append(pl.BlockSpec((d, 2 * LANES), lambda i: (0, 0)))
        out_shape += [jax.ShapeDtypeStruct((m * chunks, LANES), F32), jax.ShapeDtypeStruct((m, LANES), F32)]
        out_specs += [pl.BlockSpec((tm * chunks, LANES), lambda i: (i, 0)),
                      pl.BlockSpec((tm, LANES), lambda i: (i, 0))]
    return pl.pallas_call(
        functools.partial(_outproj_kernel, n_exp=n_exp, nct=nct),
        grid=(m // tm,), in_specs=specs, out_specs=out_specs, out_shape=out_shape,
        compiler_params=_params(("arbitrary",), 2 * tm * d * (4 * 2 + 3 * 4) + 2 * d * d * 2 + 8 * tm * d * 4),
        name="outproj_route" if n_exp else "outproj",
    )(*args)


def _changed(te_ref, i):
    return (i == 0) | (te_ref[i] != te_ref[jnp.maximum(i - 1, 0)])


def _up_kernel(te_ref, nu_ref, x_ref, wg_ref, wu_ref, o_ref, wgb, wub):
    i = pl.program_id(1)

    @pl.when(_changed(te_ref, i))
    def _():
        wgb[...] = wg_ref[...].astype(BF16)
        wub[...] = wu_ref[...].astype(BF16)

    @pl.when(i < nu_ref[0])
    def _():
        x = x_ref[...]
        a = jnp.dot(x, wgb[...], preferred_element_type=F32)
        b = jnp.dot(x, wub[...], preferred_element_type=F32)
        o_ref[...] = (a * _sigmoid(a) * b).astype(BF16)

    @pl.when(i >= nu_ref[0])
    def _():
        o_ref[...] = jnp.zeros(o_ref.shape, BF16)


def _down_kernel(te_ref, nu_ref, a_ref, w_ref, o_ref, wb):
    i = pl.program_id(1)

    @pl.when(_changed(te_ref, i))
    def _():
        wb[...] = w_ref[...].astype(BF16)

    o_ref[...] = jnp.dot(a_ref[...], wb[...], preferred_element_type=F32)


def _down_tok_kernel(te_ref, nu_ref, a_ref, w_ref, o_ref, *, chunks):
    i, j = pl.program_id(0), pl.program_id(1)
    tm = a_ref.shape[0]
    per = w_ref.shape[1] // LANES

    @pl.when(i < nu_ref[0])
    def _():
        acc = jnp.dot(a_ref[...], w_ref[...], preferred_element_type=F32)
        for jj in range(chunks // per):
            @pl.when(j == jj)
            def _():
                for c in range(per):
                    o_ref[pl.ds(jj * per + c, tm, stride=chunks), :] = acc[:, c * LANES:(c + 1) * LANES]

    @pl.when((i >= nu_ref[0]) & (j == 0))
    def _():
        o_ref[...] = jnp.zeros(o_ref.shape, F32)


def _swiglu(xs, tile_expert, n_used, tm, w_gate, w_up, w_down, sl, *, token_major):
    r, d = xs.shape
    f = w_gate.shape[3]
    tf = _tile(f, 512, LANES)
    tn = _tile(d, 512, LANES)
    nt = r // tm
    used = lambda i, nu: jnp.minimum(i, nu[0] - 1)
    act = pl.pallas_call(
        _up_kernel,
        grid_spec=pltpu.PrefetchScalarGridSpec(
            num_scalar_prefetch=2, grid=(f // tf, nt),
            in_specs=[pl.BlockSpec((tm, d), lambda j, i, te, nu: (used(i, nu), 0)),
                      pl.BlockSpec((None, None, d, tf), lambda j, i, te, nu: (sl, te[i], 0, j)),
                      pl.BlockSpec((None, None, d, tf), lambda j, i, te, nu: (sl, te[i], 0, j))],
            out_specs=pl.BlockSpec((tm, tf), lambda j, i, te, nu: (i, j)),
            scratch_shapes=[pltpu.VMEM((d, tf), BF16), pltpu.VMEM((d, tf), BF16)]),
        out_shape=jax.ShapeDtypeStruct((r, f), BF16),
        compiler_params=_params(("arbitrary", "arbitrary"),
                                2 * (tm * d * 2 + 2 * d * tf * 4 + tm * tf * 2) + 2 * d * tf * 2 + 4 * tm * tf * 4),
        name="swiglu_up",
    )(tile_expert, n_used, xs, w_gate, w_up)
    if not token_major:
        return pl.pallas_call(
            _down_kernel,
            grid_spec=pltpu.PrefetchScalarGridSpec(
                num_scalar_prefetch=2, grid=(d // tn, nt),
                in_specs=[pl.BlockSpec((tm, f), lambda j, i, te, nu: (i, 0)),
                          pl.BlockSpec((None, None, f, tn), lambda j, i, te, nu: (sl, te[i], 0, j))],
                out_specs=pl.BlockSpec((tm, tn), lambda j, i, te, nu: (i, j)),
                scratch_shapes=[pltpu.VMEM((f, tn), BF16)]),
            out_shape=jax.ShapeDtypeStruct((r, d), F32),
            compiler_params=_params(("arbitrary", "arbitrary"),
                                    2 * (tm * f * 2 + f * tn * 4 + tm * tn * 4) + f * tn * 2 + 2 * tm * tn * 4),
            name="swiglu_down",
        )(tile_expert, n_used, act, w_down)
    chunks = d // LANES
    tn = _tile(d, 1024, LANES)
    n_j = d // tn
    col = lambda i, j, nu: jnp.where(i < nu[0], j, n_j - 1)
    return pl.pallas_call(
        functools.partial(_down_tok_kernel, chunks=chunks),
        grid_spec=pltpu.PrefetchScalarGridSpec(
            num_scalar_prefetch=2, grid=(nt, n_j),
            in_specs=[pl.BlockSpec((tm, f), lambda i, j, te, nu: (used(i, nu), 0)),
                      pl.BlockSpec((None, None, f, tn), lambda i, j, te, nu: (sl, te[i], 0, col(i, j, nu)))],
            out_specs=pl.BlockSpec((tm * chunks, LANES), lambda i, j, te, nu: (i, 0))),
        out_shape=jax.ShapeDtypeStruct((r * chunks, LANES), F32),
        compiler_params=_params(("arbitrary", "arbitrary"),
                                2 * (tm * f * 2 + f * tn * 2 + tm * d * 4) + 4 * tm * tn * 4),
        name="swiglu_down_tok",
    )(tile_expert, n_used, act, w_down)


def _pitch(chunks):
    return chunks + SUBLANES


def _row_copy(src_hbm, src_row, dst, dst_row, sem, chunks):
    return pltpu.make_async_copy(
        src_hbm.at[pl.ds(pl.multiple_of(src_row * chunks, chunks), chunks), :],
        dst.at[pl.ds(pl.multiple_of(dst_row * _pitch(chunks), SUBLANES), chunks), :], sem)


def _start_rows(idx_ref, base, stride, n, src_hbm, dst, sem, chunks):
    def body(o, c):
        for u in range(DMA_UNROLL):
            r = o * DMA_UNROLL + u
            _row_copy(src_hbm, idx_ref[base + stride * r], dst, r, sem, chunks).start()
        return c
    lax.fori_loop(0, n // DMA_UNROLL, body, 0)


def _wait_rows(n, src_hbm, dst, sem, chunks):
    def body(o, c):
        for u in range(DMA_UNROLL):
            _row_copy(src_hbm, 0, dst, o * DMA_UNROLL + u, sem, chunks).wait()
        return c
    lax.fori_loop(0, n // DMA_UNROLL, body, 0)


def _gather_kernel(rt_ref, h_hbm, o_ref, buf, sems, *, chunks):
    i, n = pl.program_id(0), pl.num_programs(0)
    tg = o_ref.shape[0]
    slot = i % 2

    @pl.when(i == 0)
    def _():
        _start_rows(rt_ref, 0, 1, tg, h_hbm, buf.at[0], sems.at[0], chunks)

    @pl.when(i + 1 < n)
    def _():
        _start_rows(rt_ref, (i + 1) * tg, 1, tg, h_hbm, buf.at[1 - slot], sems.at[1 - slot], chunks)

    cur = buf.at[slot]
    _wait_rows(tg, h_hbm, cur, sems.at[slot], chunks)
    for s in range(chunks):
        o_ref[:, s * LANES:(s + 1) * LANES] = cur[pl.ds(s, tg, stride=_pitch(chunks)), :].astype(BF16)


def _gather_rows(h_tok, row_token, d):
    chunks = d // LANES
    r = row_token.shape[0]
    tg = _tile(r, 256, SUBLANES * DMA_UNROLL)
    return pl.pallas_call(
        functools.partial(_gather_kernel, chunks=chunks),
        grid_spec=pltpu.PrefetchScalarGridSpec(
            num_scalar_prefetch=1, grid=(r // tg,),
            in_specs=[pl.BlockSpec(memory_space=pl.ANY)],
            out_specs=pl.BlockSpec((tg, d), lambda i, rt: (i, 0)),
            scratch_shapes=[pltpu.VMEM((2, tg * _pitch(chunks), LANES), F32), pltpu.SemaphoreType.DMA((2,))]),
        out_shape=jax.ShapeDtypeStruct((r, d), BF16),
        compiler_params=_params(("arbitrary",), 2 * tg * d * 4 + 2 * tg * d * 2 + 2 * tg * d * 4),
        name="moe_gather",
    )(row_token, h_tok)


def _combine_kernel(*refs, chunks, final, nct):
    pos_ref, o_hbm, rt_ref, x_ref, g_ref, n_ref = refs[:6]
    rest = list(refs[6:])
    if not final:
        sc_ref, sh_ref = rest[:2]
        rest = rest[2:]
    out_a, out_b, bufs, sems = rest
    i, n = pl.program_id(0), pl.num_programs(0)
    tc = x_ref.shape[0]
    slot = i % 2

    def start(step, sl):
        for k in range(TOP_K):
            _start_rows(pos_ref, TOP_K * step * tc + k, TOP_K, tc, o_hbm, bufs.at[sl, k], sems.at[sl, k], chunks)

    @pl.when(i == 0)
    def _():
        start(0, 0)

    @pl.when(i + 1 < n)
    def _():
        start(i + 1, 1 - slot)

    for k in range(TOP_K):
        _wait_rows(tc, o_hbm, bufs.at[slot, k], sems.at[slot, k], chunks)
    w0 = rt_ref[:, TOP_K:TOP_K + 1]
    w1 = rt_ref[:, TOP_K + 1:TOP_K + 2]
    b0, b1 = bufs.at[slot, 0], bufs.at[slot, 1]
    pitch = _pitch(chunks)
    cols = [w0 * b0[pl.ds(s, tc, stride=pitch), :] + w1 * b1[pl.ds(s, tc, stride=pitch), :]
            for s in range(chunks)]
    x = x_ref[...] + g_ref[...] * jnp.concatenate(cols, axis=1)
    y = _rms(x) * n_ref[...]
    if final:
        @pl.when(i < nct)
        def _():
            out_a[...] = y

        @pl.when(i >= nct)
        def _():
            out_b[...] = y
    else:
        out_a[...] = x
        out_b[...] = (y * (1.0 + sc_ref[...]) + sh_ref[...]).astype(BF16)


def _combine(o_tok, pos, route, x, g, n, sc, sh, grp, n_ctx, *, final):
    m, d = x.shape
    chunks = d // LANES
    tc = _tile(math.gcd(m, n_ctx), 256, SUBLANES * DMA_UNROLL)
    nct = n_ctx // tc
    row = pl.BlockSpec((tc, d), lambda i, p: (i, 0))
    gspec = pl.BlockSpec((None, 1, d), lambda i, p: (grp(i), 0, 0))
    args = [pos, o_tok, route, x, g, n.reshape(1, d)]
    specs = [pl.BlockSpec(memory_space=pl.ANY), pl.BlockSpec((tc, LANES), lambda i, p: (i, 0)), row, gspec,
             pl.BlockSpec((1, d), lambda i, p: (0, 0))]
    if not final:
        args += [sc, sh]
        specs += [gspec, gspec]
    if final:
        out_specs = [pl.BlockSpec((tc, d), lambda i, p: (jnp.minimum(i, nct - 1), 0)),
                     pl.BlockSpec((tc, d), lambda i, p: (jnp.maximum(i - nct, 0), 0))]
        out_shape = [jax.ShapeDtypeStruct((n_ctx, d), F32), jax.ShapeDtypeStruct((m - n_ctx, d), F32)]
    else:
        out_specs = [row, row]
        out_shape = [jax.ShapeDtypeStruct((m, d), F32), jax.ShapeDtypeStruct((m, d), BF16)]
    return pl.pallas_call(
        functools.partial(_combine_kernel, chunks=chunks, final=final, nct=nct),
        grid_spec=pltpu.PrefetchScalarGridSpec(
            num_scalar_prefetch=1, grid=(m // tc,), in_specs=specs,
            out_specs=out_specs,
            scratch_shapes=[pltpu.VMEM((2, TOP_K, tc * _pitch(chunks), LANES), F32),
                            pltpu.SemaphoreType.DMA((2, TOP_K))]),
        out_shape=out_shape,
        compiler_params=_params(("arbitrary",), 2 * TOP_K * tc * d * 4 + 6 * tc * d * 4 + 6 * tc * d * 4),
        name="moe_combine",
    )(*args)


def _route_plan(route, n_exp, tm):
    m = route.shape[0]
    flat_e = route[:, :TOP_K].astype(jnp.int32).reshape(-1)
    onehot = (flat_e[:, None] == jnp.arange(n_exp)[None, :]).astype(jnp.int32)
    csum = jnp.cumsum(onehot, axis=0)
    rank = jnp.take_along_axis(csum, flat_e[:, None], axis=1)[:, 0] - 1
    counts = csum[-1]
    padded = (counts + tm - 1) // tm * tm
    starts = jnp.cumsum(padded) - padded
    pos = starts[flat_e] + rank
    n_rows = TOP_K * m + n_exp * tm
    row_token = jnp.zeros((n_rows,), jnp.int32).at[pos].set(jnp.arange(TOP_K * m, dtype=jnp.int32) // TOP_K)
    tiles = jnp.arange(n_rows // tm, dtype=jnp.int32)
    tile_expert = jnp.sum(tiles[:, None] * tm >= starts[None, :], axis=1).astype(jnp.int32) - 1
    n_used = (jnp.sum(padded) // tm).astype(jnp.int32).reshape(1)
    return pos.astype(jnp.int32), row_token, tile_expert, n_used


def _rope_tables(dec_seq, hd):
    half = hd // 4
    rows = dec_seq // GRID_W
    row_pos = jnp.broadcast_to(jnp.arange(rows)[:, None], (rows, GRID_W)).reshape(-1).astype(F32)
    col_pos = jnp.broadcast_to(jnp.arange(GRID_W)[None, :], (rows, GRID_W)).reshape(-1).astype(F32)
    inv_freq = ROPE_THETA ** (-jnp.arange(0, 2 * half, 2, dtype=F32) / (2 * half))
    ang_r = row_pos[:, None] * inv_freq[None, :]
    ang_c = col_pos[:, None] * inv_freq[None, :]
    cos = jnp.concatenate([jnp.cos(ang_r)] * 2 + [jnp.cos(ang_c)] * 2, axis=1)
    sin = jnp.concatenate([-jnp.sin(ang_r), jnp.sin(ang_r), -jnp.sin(ang_c), jnp.sin(ang_c)], axis=1)
    return cos, sin


def kernel(x_prompt, x_sample, c, cache_k, cache_v, c_ctx, w_mod, b_mod, norm1_g, norm2_g, w_in,
           q_norm_g, k_norm_g, w_pool, pool_scale, w_out, dense_w_gate, dense_w_up, dense_w_down,
           router_w, moe_w_gate, moe_w_up, moe_w_down, final_norm_g):
    batch, seq, d = x_prompt.shape
    dec_batch, dec_seq, _ = x_sample.shape
    depth = w_mod.shape[0]
    n_kv, hd = cache_k.shape[3], cache_k.shape[4]
    kvw = n_kv * hd
    qw = d
    pw = pool_scale.shape[1]
    n_ctx = batch * seq
    m = n_ctx + dec_batch * dec_seq
    dims = dict(n_ctx=n_ctx, seq=seq, dec_seq=dec_seq, qw=qw, kvw=kvw, pw=pw, hd=hd)
    n_exp = router_w.shape[2]
    gw = w_in.shape[2] - qw - 2 * kvw - pw
    assert hd == LANES and d % LANES == 0 and gw == 2 * d

    def grp_for(tile_rows):
        return lambda i: jnp.maximum((i * tile_rows - n_ctx) // dec_seq + 1, 0)

    n_grp = 1 + dec_batch
    rows = -(-n_grp // SUBLANES) * SUBLANES
    cs = jnp.zeros((rows, d), F32).at[0].set(c_ctx).at[1:n_grp].set(c)
    mod = _modulation(cs, w_mod, b_mod)[:, :n_grp]
    mod = mod.reshape(depth, n_grp, N_MOD, 1, d).transpose(0, 2, 1, 3, 4)

    tabs = _rope_tables(dec_seq, hd)
    bands = _pool_bands(min(POOL_BLOCK, seq))
    w_out_b = w_out.astype(BF16)
    moe_w_down_b = moe_w_down.astype(BF16)

    tm_rn = _tile(math.gcd(m, n_ctx), 512, SUBLANES)
    tm_op = _tile(math.gcd(m, n_ctx), 256, SUBLANES)
    assert tm_rn == _tile(m, 512, SUBLANES) and dec_seq % tm_rn == 0 and dec_seq % tm_op == 0
    x, h = _embed_norm(x_prompt.reshape(n_ctx, d), x_sample.reshape(-1, d), norm1_g[0], mod[0, 1], mod[0, 0],
                       grp_for(tm_rn))

    new_k, new_v = [], []
    for l in range(depth):
        sh1, sc1, g1, sh2, sc2, g2 = (mod[l, k] for k in range(N_MOD))
        q = _proj(h, w_in, l, 0, qw, dims, mode="q", gain=q_norm_g[l], tabs=tabs)
        k, kc = _proj(h, w_in, l, qw, kvw, dims, mode="k", gain=k_norm_g[l], tabs=tabs, with_cache=True)
        v, vc = _proj(h, w_in, l, qw + kvw, kvw, dims, mode="v", with_cache=True)
        u = _proj(h, w_in, l, qw + 2 * kvw, pw, dims, mode="u")
        gates = _proj(h, w_in, l, qw + 2 * kvw + pw, gw, dims, mode="gate")
        new_k.append(kc.reshape(batch, seq, n_kv, hd))
        new_v.append(vc.reshape(batch, seq, n_kv, hd))
        attn_ctx, attn_lat = _attention(q, k, v, cache_k, cache_v, l, dims)
        pool = _pool(u, bands, w_pool, pool_scale[l:l + 1], l, dims)
        last = l == depth - 1
        if last:
            n_next, sc_next, sh_next = final_norm_g, None, None
        else:
            n_next, sc_next, sh_next = norm1_g[l + 1], mod[l + 1, 1], mod[l + 1, 0]
        j = l // 2
        if l % 2 == 0:
            x, h2 = _outproj(attn_ctx, attn_lat, pool, gates, x, w_out_b[l], g1, norm2_g[l], sc2, sh2, grp_for(tm_op), None)
            tm = _tile(m, DENSE_FFN_ROWS, SUBLANES)
            nt = jnp.full((1,), m // tm, jnp.int32)
            o = _swiglu(h2, jnp.zeros((m // tm,), jnp.int32), nt, tm,
                        dense_w_gate[:, None], dense_w_up[:, None], dense_w_down[:, None], j, token_major=False)
            x, h = _resid_norm(x, o, g2, n_next, sc_next, sh_next, grp_for(tm_rn), final=last)
        else:
            x, h2_tok, route = _outproj(attn_ctx, attn_lat, pool, gates, x, w_out_b[l], g1, norm2_g[l], sc2, sh2,
                                        grp_for(tm_op), router_w[j])
            tm = _tile(m, MOE_ROWS, SUBLANES)
            pos, row_token, tile_expert, n_used = _route_plan(route, n_exp, tm)
            xs = _gather_rows(h2_tok, row_token, d)
            o_tok = _swiglu(xs, tile_expert, n_used, tm, moe_w_gate, moe_w_up, moe_w_down_b, j, token_major=True)
            x, h = _combine(o_tok, pos, route, x, g2, n_next, sc_next, sh_next, grp_for(tm_op), n_ctx, final=last)
            if last:
                y_ctx, y_lat = x, h

    if depth % 2 == 1:
        y_ctx, y_lat = h[:n_ctx], h[n_ctx:]
    y_prompt = y_ctx.reshape(batch, seq, d)
    y_sample = y_lat.reshape(dec_batch, dec_seq, d)
    return (y_prompt, y_sample, jnp.stack(new_k, axis=1), jnp.stack(new_v, axis=1))
```
